```python
import math
import jax, jax.numpy as jnp
from jax import lax
import numpy as np

D_MODEL = 1024
BATCH = 16
SEQ = 4096
DEPTH = 2
DEC_BATCH = 32
DEC_SEQ = 32
PAST_LEN = 2048

CHUNK = 64
N_A_LAYERS = max(1, DEPTH // 2)
N_B_LAYERS = DEPTH - N_A_LAYERS
CONV_W = 3
HEAD_DIM = 64
N_HEADS = D_MODEL // (2 * HEAD_DIM)
ATT_WIDTH = N_HEADS * 2 * HEAD_DIM
Q_BLOCK = 128
PEER_HEADS = 8
N_KEYS = 128
N_EXPERTS = N_KEYS * N_KEYS
PEER_TOPK = 16
PEER_QDIM = 256
D_HALF = PEER_QDIM // 2
PEER_BLOCK = 256
EPS = 1e-6

kernel_name = "yoco_shortconv_diffattn_peer_stream_step"


def _rms(x, g):
    xf = x.astype(jnp.float32)
    y = xf * lax.rsqrt(jnp.mean(xf * xf, axis=-1, keepdims=True) + EPS)
    return (y * g).astype(x.dtype)


def _lambda_init(layer_idx):
    return 0.8 - 0.6 * math.exp(-0.3 * layer_idx)


def _short_conv(h, state, w_in, conv_w, w_out):
    S = h.shape[1]
    b_gate, c_gate, u = jnp.split(h @ w_in, 3, axis=-1)
    z = c_gate * u
    if state is None:
        state = jnp.zeros((z.shape[0], CONV_W - 1, z.shape[2]), z.dtype)
    zp = jnp.concatenate([state.astype(z.dtype), z], axis=1)
    conv = sum(conv_w[j] * zp[:, j:j + S] for j in range(CONV_W))
    new_state = zp[:, -(CONV_W - 1):]
    return (b_gate * conv) @ w_out, new_state


def _diff_block(q, k, v, mask, lam):
    s = jnp.einsum('qhcd,khcd->hcqk', q, k).astype(jnp.float32) * (HEAD_DIM ** -0.5)
    s = jnp.where(mask, s, -jnp.inf)
    p = jax.nn.softmax(s, axis=-1)
    a = p[:, 0] - lam * p[:, 1]
    return jnp.einsum('hqk,khe->qhe', a.astype(v.dtype), v)


def _attn_prompt(q, k, v, lam):
    S = q.shape[1]
    nq = S // Q_BLOCK
    k_chunk = jnp.arange(S) // CHUNK

    def per_seq(args):
        qs, ks, vs = args
        qb = qs.reshape(nq, Q_BLOCK, N_HEADS, 2, HEAD_DIM)

        def per_block(a):
            i, qblk = a
            q_chunk = (i * Q_BLOCK + jnp.arange(Q_BLOCK)) // CHUNK
            mask = k_chunk[None, :] <= q_chunk[:, None]
            return _diff_block(qblk, ks, vs, mask, lam)

        o = lax.map(per_block, (jnp.arange(nq), qb))
        return o.reshape(S, N_HEADS, 2 * HEAD_DIM)

    return lax.map(per_seq, (q, k, v))


def _attn_sample(q, k_all, v_all, lam):
    mask = jnp.ones((q.shape[1], k_all.shape[1]), bool)
    return jax.vmap(lambda qs, ks, vs: _diff_block(qs, ks, vs, mask, lam))(q, k_all, v_all)


def _diff_attn(h, k_all, v_all, prompt, wq, lam_p, subln, wo, lam_init):
    B, S, _ = h.shape
    q = (h @ wq).reshape(B, S, N_HEADS, 2, HEAD_DIM)
    lp = lam_p.astype(jnp.float32)
    lam = jnp.exp(jnp.sum(lp[0] * lp[1])) - jnp.exp(jnp.sum(lp[2] * lp[3])) + lam_init
    if prompt:
        o = _attn_prompt(q, k_all, v_all, lam)
    else:
        o = _attn_sample(q, k_all, v_all, lam)
    o = _rms(o, subln) * (1.0 - lam_init)
    return o.reshape(B, S, ATT_WIDTH) @ wo


def _peer(x, wq, keys, u, v):
    shp = x.shape
    xt = x.reshape(-1, D_MODEL)
    T = xt.shape[0]
    nb = -(-T // PEER_BLOCK)
    xt = jnp.pad(xt, ((0, nb * PEER_BLOCK - T), (0, 0))).reshape(nb, PEER_BLOCK, D_MODEL)

    def block(xb):
        t = xb.shape[0]
        q = (xb @ wq).reshape(t, PEER_HEADS, 2, D_HALF)
        s = jnp.einsum('thcd,hcnd->thcn', q, keys).astype(jnp.float32)
        sv, si = lax.top_k(s, PEER_TOPK)
        cand_s = (sv[:, :, 0, :, None] + sv[:, :, 1, None, :]).reshape(t, PEER_HEADS, PEER_TOPK * PEER_TOPK)
        cand_i = (si[:, :, 0, :, None] * N_KEYS + si[:, :, 1, None, :]).reshape(t, PEER_HEADS, PEER_TOPK * PEER_TOPK)
        top_s, pos = lax.top_k(cand_s, PEER_TOPK)
        expert = jnp.take_along_axis(cand_i, pos, axis=-1)
        g = jax.nn.softmax(top_s, axis=-1)
        ue = jnp.take(u, expert, axis=0)
        act = jax.nn.gelu(jnp.einsum('thkd,td->thk', ue, xb).astype(jnp.float32), approximate=False)
        w = (g * act).astype(xb.dtype)
        ve = jnp.take(v, expert, axis=0)
        return jnp.einsum('thk,thkd->td', w, ve)

    out = lax.map(block, xt).reshape(nb * PEER_BLOCK, D_MODEL)[:T]
    return out.reshape(shp)


def _trunk(x, conv_state, past_k, past_v, prompt, g_mix, g_ffn, conv_w_in, conv_w, conv_w_out,
           g_kv, attn_wk, attn_wv, attn_wq, attn_lambda, attn_subln, attn_wo,
           peer_wq, peer_keys, peer_u, peer_v, g_final):
    B, S, _ = x.shape
    new_conv = []
    k_new = v_new = k_all = v_all = None
    for l in range(DEPTH):
        h = _rms(x, g_mix[l])
        if l < N_A_LAYERS:
            st = None if prompt else conv_state[l]
            y, st_new = _short_conv(h, st, conv_w_in[l], conv_w[l], conv_w_out[l])
            new_conv.append(st_new)
        else:
            j = l - N_A_LAYERS
            y = _diff_attn(h, k_all, v_all, prompt, attn_wq[j], attn_lambda[j],
                           attn_subln[j], attn_wo[j], _lambda_init(l))
        x = x + y
        x = x + _peer(_rms(x, g_ffn[l]), peer_wq[l], peer_keys[l], peer_u[l], peer_v[l])
        if l == N_A_LAYERS - 1:
            hk = _rms(x, g_kv)
            k_new = (hk @ attn_wk).reshape(B, S, N_HEADS, 2, HEAD_DIM)
            v_new = (hk @ attn_wv).reshape(B, S, N_HEADS, 2 * HEAD_DIM)
            if prompt:
                k_all, v_all = k_new, v_new
            else:
                k_all = jnp.concatenate([past_k.astype(k_new.dtype), k_new], axis=1)
                v_all = jnp.concatenate([past_v.astype(v_new.dtype), v_new], axis=1)
    return _rms(x, g_final), k_new, v_new, jnp.stack(new_conv, axis=0)


def setup_inputs(seed: int = 0) -> dict:
    key = jax.random.key(seed)
    ks = jax.random.split(key, 22)

    def nrm(k, shape, scale):
        return jax.random.normal(k, shape, jnp.float32) * scale

    D = D_MODEL
    return {
        "x_prompt": nrm(ks[0], (BATCH, SEQ, D), 1.0),
        "x_sample": nrm(ks[1], (DEC_BATCH, DEC_SEQ, D), 1.0),
        "cache_k": nrm(ks[2], (DEC_BATCH, PAST_LEN, N_HEADS, 2, HEAD_DIM), 1.0),
        "cache_v": nrm(ks[3], (DEC_BATCH, PAST_LEN, N_HEADS, 2 * HEAD_DIM), 1.0),
        "state_conv": nrm(ks[4], (N_A_LAYERS, DEC_BATCH, CONV_W - 1, D), 1.0),
        "g_mix": 1.0 + nrm(ks[5], (DEPTH, D), 0.05),
        "g_ffn": 1.0 + nrm(ks[6], (DEPTH, D), 0.05),
        "conv_w_in": nrm(ks[7], (N_A_LAYERS, D, 3 * D), D ** -0.5),
        "conv_w": nrm(ks[8], (N_A_LAYERS, CONV_W, D), CONV_W ** -0.5),
        "conv_w_out": nrm(ks[9], (N_A_LAYERS, D, D), D ** -0.5),
        "g_kv": 1.0 + nrm(ks[10], (D,), 0.05),
        "attn_wk": nrm(ks[11], (D, ATT_WIDTH), D ** -0.5),
        "attn_wv": nrm(ks[12], (D, ATT_WIDTH), D ** -0.5),
        "attn_wq": nrm(ks[13], (N_B_LAYERS, D, ATT_WIDTH), D ** -0.5),
        "attn_lambda": nrm(ks[14], (N_B_LAYERS, 4, HEAD_DIM), 0.1),
        "attn_subln": 1.0 + nrm(ks[15], (N_B_LAYERS, 2 * HEAD_DIM), 0.05),
        "attn_wo": nrm(ks[16], (N_B_LAYERS, ATT_WIDTH, D), ATT_WIDTH ** -0.5),
        "peer_wq": nrm(ks[17], (DEPTH, D, PEER_HEADS * PEER_QDIM), D ** -0.5),
        "peer_keys": nrm(ks[18], (DEPTH, PEER_HEADS, 2, N_KEYS, D_HALF), D_HALF ** -0.5),
        "peer_u": nrm(ks[19], (DEPTH, N_EXPERTS, D), D ** -0.5),
        "peer_v": nrm(ks[20], (DEPTH, N_EXPERTS, D), PEER_HEADS ** -0.5),
        "g_final": 1.0 + nrm(ks[21], (D,), 0.05),
    }


def reference(x_prompt, x_sample, cache_k, cache_v, state_conv, g_mix, g_ffn, conv_w_in, conv_w,
              conv_w_out, g_kv, attn_wk, attn_wv, attn_wq, attn_lambda, attn_subln, attn_wo,
              peer_wq, peer_keys, peer_u, peer_v, g_final):
    w = (g_mix, g_ffn, conv_w_in, conv_w, conv_w_out, g_kv, attn_wk, attn_wv, attn_wq,
         attn_lambda, attn_subln, attn_wo, peer_wq, peer_keys, peer_u, peer_v, g_final)
    y_prompt, k_prompt, v_prompt, conv_prompt = _trunk(x_prompt, None, None, None, True, *w)
    y_sample, k_sample, v_sample, conv_sample = _trunk(x_sample, state_conv, cache_k, cache_v, False, *w)
    return (y_prompt, y_sample, k_prompt, v_prompt, conv_prompt, k_sample, v_sample, conv_sample)
```

```python
import functools
import math

import jax
import jax.numpy as jnp
from jax import lax
from jax.experimental import pallas as pl
from jax.experimental.pallas import tpu as pltpu

D_MODEL = 1024
CHUNK = 64
CONV_W = 3
HEAD_DIM = 64
N_HEADS = 8
HEAD_W = 2 * HEAD_DIM
PEER_HEADS = 8
N_KEYS = 128
N_EXPERTS = N_KEYS * N_KEYS
PEER_TOPK = 16
PEER_QDIM = 256
EPS = 1e-6
LAMBDA_INIT_L1 = 0.8 - 0.6 * math.exp(-0.3 * 1)

LANES = 128
VMEM_LIMIT = 56 * 1024 * 1024

F32 = jnp.float32
BF16 = jnp.bfloat16
NEG_INF = float("-inf")


def _rms_rows(x, g):
    return x * lax.rsqrt(jnp.mean(x * x, axis=-1, keepdims=True) + EPS) * g


def _conv_kernel(x_ref, st_ref, g_ref, win_ref, cw_ref, wout_ref, o_ref, ns_ref, zp_ref, *, ts):
    s = pl.program_id(1)
    d = D_MODEL

    @pl.when(s == 0)
    def _():
        zp_ref[0:8, :] = jnp.zeros((8, d), F32)
        zp_ref[6:8, :] = st_ref[0]

    x = x_ref[0]
    h = _rms_rows(x, g_ref[...]).astype(BF16)
    p = jnp.dot(h, win_ref[...], preferred_element_type=F32)
    z = p[:, d:2 * d] * p[:, 2 * d:]
    zp_ref[8:8 + ts, :] = z
    conv = (cw_ref[0:1, :] * zp_ref[6:6 + ts, :] + cw_ref[1:2, :] * zp_ref[7:7 + ts, :]
            + cw_ref[2:3, :] * z)
    y = jnp.dot((p[:, :d] * conv).astype(BF16), wout_ref[...], preferred_element_type=F32)
    o_ref[0] = x + y
    ns_ref[0] = zp_ref[ts + 6:ts + 8, :]
    zp_ref[0:8, :] = zp_ref[ts:ts + 8, :]


def _conv_layer(x, state, g, w_in, conv_w, w_out):
    b, s, d = x.shape
    ts = min(s, 512)
    return pl.pallas_call(
        functools.partial(_conv_kernel, ts=ts),
        grid=(b, s // ts),
        in_specs=[
            pl.BlockSpec((1, ts, d), lambda i, j: (i, j, 0)),
            pl.BlockSpec((1, CONV_W - 1, d), lambda i, j: (i, 0, 0)),
            pl.BlockSpec((1, d), lambda i, j: (0, 0)),
            pl.BlockSpec((d, 3 * d), lambda i, j: (0, 0)),
            pl.BlockSpec((CONV_W, d), lambda i, j: (0, 0)),
            pl.BlockSpec((d, d), lambda i, j: (0, 0)),
        ],
        out_specs=[
            pl.BlockSpec((1, ts, d), lambda i, j: (i, j, 0)),
            pl.BlockSpec((1, CONV_W - 1, d), lambda i, j: (i, 0, 0)),
        ],
        out_shape=[
            jax.ShapeDtypeStruct((b, s, d), F32),
            jax.ShapeDtypeStruct((b, CONV_W - 1, d), F32),
        ],
        scratch_shapes=[pltpu.VMEM((ts + 8, d), F32)],
        compiler_params=pltpu.CompilerParams(
            dimension_semantics=("arbitrary", "arbitrary"), vmem_limit_bytes=VMEM_LIMIT),
        name="conv_layer",
    )(x, state, g, w_in, conv_w, w_out)


def _top16(s, want_rank):
    vals = []
    rank = jnp.full(s.shape, float(PEER_TOPK), F32) if want_rank else None
    for a in range(PEER_TOPK):
        m = jnp.max(s, axis=0, keepdims=True)
        hit = s == m
        if want_rank:
            rank = jnp.where(hit, float(a), rank)
        s = jnp.where(hit, NEG_INF, s)
        vals.append(m)
    return vals, rank


def _peer_route(s0, s1):
    v0, _ = _top16(s0, False)
    v1, rank1 = _top16(s1, True)
    v0a = jnp.concatenate(v0, axis=0)
    v1a = jnp.concatenate(v1, axis=0)
    cand = jnp.concatenate(
        [v0[0] + v1a] + [v0[a] + v1a[0:8] for a in range(1, 8)] + [v0a[8:16] + v1[0]], axis=0)
    top = v0[0] + v1[0]
    c = cand
    tau = top
    for _ in range(PEER_TOPK):
        tau = jnp.max(c, axis=0, keepdims=True)
        c = jnp.where(c == tau, NEG_INF, c)
    z = jnp.sum(jnp.where(cand >= tau, jnp.exp(cand - top), 0.0), axis=0, keepdims=True)
    cnt = jnp.zeros(s0.shape, F32)
    for b in range(PEER_TOPK):
        cnt = cnt + jnp.where(s0 + v1[b] >= tau, 1.0, 0.0)
    a_w = jnp.exp(s0 - v0[0]) / z
    b_w = jnp.exp(s1 - v1[0])
    return cnt, rank1, a_w, b_w


def _peer_kernel(x_ref, g_ref, gf_ref, wqt_ref, keys_ref, u_ref, vt_ref, o_ref,
                 ht_ref, sc_ref, rk_ref, bw_ref, cn_ref, aw_ref, acc_ref, a_ref, w_ref,
                 *, tm, ne, final):
    e = pl.program_id(1)
    n_e = pl.num_programs(1)
    rows_per_step = ne // N_KEYS

    @pl.when(e == 0)
    def _prologue():
        h = _rms_rows(x_ref[...], g_ref[...])
        ht_ref[...] = h.T.astype(BF16)
        qt = jnp.dot(wqt_ref[...], ht_ref[...], preferred_element_type=F32).astype(BF16)
        for hc in range(2 * PEER_HEADS):
            sc_ref[hc] = jnp.dot(keys_ref[hc], qt[hc * N_KEYS:(hc + 1) * N_KEYS],
                                 preferred_element_type=F32)
        n_chunks = tm // LANES

        def route_body(idx, carry):
            hd = idx // n_chunks
            ln = pl.ds(pl.multiple_of((idx % n_chunks) * LANES, LANES), LANES)
            cnt, rank1, a_w, b_w = _peer_route(sc_ref[2 * hd, :, ln], sc_ref[2 * hd + 1, :, ln])
            cn_ref[hd, :, ln] = cnt
            aw_ref[hd, :, ln] = a_w
            rk_ref[hd, :, ln] = rank1.astype(BF16)
            bw_ref[hd, :, ln] = b_w.astype(BF16)
            return carry

        lax.fori_loop(0, PEER_HEADS * n_chunks, route_body, 0)
        acc_ref[...] = jnp.zeros(acc_ref.shape, F32)

    a_ref[...] = jnp.dot(u_ref[...], ht_ref[...], preferred_element_type=F32)

    def chunk_body(tc, carry):
        ln = pl.ds(pl.multiple_of(tc * LANES, LANES), LANES)
        for blk in range(rows_per_step // 8):
            i0 = pl.ds(pl.multiple_of(e * rows_per_step + blk * 8, 8), 8)
            c8 = [cn_ref[hd, i0, ln].astype(BF16) for hd in range(PEER_HEADS)]
            a8 = [aw_ref[hd, i0, ln].astype(BF16) for hd in range(PEER_HEADS)]
            for r in range(8):
                gate = jnp.zeros((N_KEYS, LANES), BF16)
                for hd in range(PEER_HEADS):
                    sel = jnp.where(rk_ref[hd, :, ln] < c8[hd][r:r + 1], bw_ref[hd, :, ln],
                                    jnp.zeros((), BF16))
                    gate = gate + sel * a8[hd][r:r + 1]
                rows = slice((blk * 8 + r) * N_KEYS, (blk * 8 + r + 1) * N_KEYS)
                pre = a_ref[rows, ln]
                act = 0.5 * pre * (1.0 + lax.erf(pre * (2.0 ** -0.5)))
                w_ref[rows, ln] = (gate.astype(F32) * act).astype(BF16)
        return carry

    lax.fori_loop(0, tm // LANES, chunk_body, 0)
    acc_ref[...] += jnp.dot(vt_ref[...], w_ref[...], preferred_element_type=F32)

    @pl.when(e == n_e - 1)
    def _epilogue():
        y = x_ref[...] + acc_ref[...].T
        if final:
            y = _rms_rows(y, gf_ref[...])
        o_ref[...] = y


def _peer_layer(x, g, g_final, wq_t, keys, u, v_t, *, final):
    t, d = x.shape
    tm = min(t, 512)
    ne = 1024
    return pl.pallas_call(
        functools.partial(_peer_kernel, tm=tm, ne=ne, final=final),
        grid=(t // tm, N_EXPERTS // ne),
        in_specs=[
            pl.BlockSpec((tm, d), lambda i, e: (i, 0)),
            pl.BlockSpec((1, d), lambda i, e: (0, 0)),
            pl.BlockSpec((1, d), lambda i, e: (0, 0)),
            pl.BlockSpec((PEER_HEADS * PEER_QDIM, d), lambda i, e: (0, 0)),
            pl.BlockSpec((2 * PEER_HEADS, N_KEYS, N_KEYS), lambda i, e: (0, 0, 0)),
            pl.BlockSpec((ne, d), lambda i, e: (e, 0)),
            pl.BlockSpec((d, ne), lambda i, e: (0, e)),
        ],
        out_specs=pl.BlockSpec((tm, d), lambda i, e: (i, 0)),
        out_shape=jax.ShapeDtypeStruct((t, d), F32),
        scratch_shapes=[
            pltpu.VMEM((d, tm), BF16),
            pltpu.VMEM((2 * PEER_HEADS, N_KEYS, tm), F32),
            pltpu.VMEM((PEER_HEADS, N_KEYS, tm), BF16),
            pltpu.VMEM((PEER_HEADS, N_KEYS, tm), BF16),
            pltpu.VMEM((PEER_HEADS, N_KEYS, tm), F32),
            pltpu.VMEM((PEER_HEADS, N_KEYS, tm), F32),
            pltpu.VMEM((d, tm), F32),
            pltpu.VMEM((ne, tm), F32),
            pltpu.VMEM((ne, tm), BF16),
        ],
        compiler_params=pltpu.CompilerParams(
            dimension_semantics=("arbitrary", "arbitrary"), vmem_limit_bytes=VMEM_LIMIT),
        name="peer_final" if final else "peer",
    )(x, g, g_final, wq_t, keys, u, v_t)


def _kvq_kernel(x_ref, gkv_ref, gq_ref, wk_ref, wv_ref, wq_ref,
                k_ref, v_ref, kb_ref, vb_ref, qb_ref):
    x = x_ref[...]
    n = x * lax.rsqrt(jnp.mean(x * x, axis=-1, keepdims=True) + EPS)
    hk = (n * gkv_ref[...]).astype(BF16)
    hq = (n * gq_ref[...]).astype(BF16)
    k = jnp.dot(hk, wk_ref[...], preferred_element_type=F32)
    v = jnp.dot(hk, wv_ref[...], preferred_element_type=F32)
    q = jnp.dot(hq, wq_ref[...], preferred_element_type=F32)
    k_ref[...] = k
    v_ref[...] = v
    kb_ref[...] = k.astype(BF16)
    vb_ref[...] = v.astype(BF16)
    qb_ref[...] = (q * (HEAD_DIM ** -0.5)).astype(BF16)


def _kvq_proj(x, g_kv, g_q, wk, wv, wq):
    t, d = x.shape
    tm = min(t, 512)
    row = pl.BlockSpec((tm, d), lambda i: (i, 0))
    vec = pl.BlockSpec((1, d), lambda i: (0, 0))
    mat = pl.BlockSpec((d, d), lambda i: (0, 0))
    return pl.pallas_call(
        _kvq_kernel,
        grid=(t // tm,),
        in_specs=[row, vec, vec, mat, mat, mat],
        out_specs=[row] * 5,
        out_shape=[jax.ShapeDtypeStruct((t, d), F32)] * 2 + [jax.ShapeDtypeStruct((t, d), BF16)] * 3,
        compiler_params=pltpu.CompilerParams(
            dimension_semantics=("arbitrary",), vmem_limit_bytes=VMEM_LIMIT),
        name="kvq_proj",
    )(x, g_kv, g_q, wk, wv, wq)


def _lambda_full(lam_ref):
    lp = lam_ref[...]
    s01 = jnp.sum(lp[0:1] * lp[1:2], axis=-1, keepdims=True)
    s23 = jnp.sum(lp[2:3] * lp[3:4], axis=-1, keepdims=True)
    return jnp.exp(s01) - jnp.exp(s23) + LAMBDA_INIT_L1


def _stack_maps(q):
    lane = lax.broadcasted_iota(jnp.int32, q.shape, 1)
    zero = jnp.zeros((), q.dtype)
    return jnp.concatenate([jnp.where(lane < HEAD_DIM, q, zero), jnp.where(lane >= HEAD_DIM, q, zero)],
                           axis=0)


def _flash_step(qs, k, v, m, l, acc, mask):
    s = lax.dot_general(qs, k, (((1,), (1,)), ((), ())), preferred_element_type=F32)
    if mask is not None:
        s = jnp.where(mask, s, NEG_INF)
    m_new = jnp.maximum(m, jnp.max(s, axis=-1, keepdims=True))
    alpha = jnp.exp(m - m_new)
    p = jnp.exp(s - m_new)
    l = alpha * l + jnp.sum(p, axis=-1, keepdims=True)
    acc = alpha * acc + jnp.dot(p.astype(BF16), v, preferred_element_type=F32)
    return m_new, l, acc


def _diff_finish(l, acc, lam, subln, n):
    o = acc / l
    o = o[:n] - lam * o[n:]
    o = o * lax.rsqrt(jnp.mean(o * o, axis=-1, keepdims=True) + EPS) * subln
    return (o * (1.0 - LAMBDA_INIT_L1)).astype(BF16)


def _attn_prompt_kernel(q_ref, k_ref, v_ref, lam_ref, sub_ref, o_ref, *, tq):
    qi = pl.program_id(2)
    qs = _stack_maps(q_ref[0])
    m0 = jnp.full((2 * tq, 1), NEG_INF, F32)
    l0 = jnp.zeros((2 * tq, 1), F32)
    acc0 = jnp.zeros((2 * tq, HEAD_W), F32)

    def body(j, carry):
        rows = pl.ds(pl.multiple_of(j * tq, tq), tq)
        return _flash_step(qs, k_ref[0, rows, :], v_ref[0, rows, :], *carry, None)

    m, l, acc = lax.fori_loop(0, qi, body, (m0, l0, acc0))
    rows = pl.ds(pl.multiple_of(qi * tq, tq), tq)
    qc = lax.broadcasted_iota(jnp.int32, (2 * tq, tq), 0) % tq // CHUNK
    kc = lax.broadcasted_iota(jnp.int32, (2 * tq, tq), 1) // CHUNK
    m, l, acc = _flash_step(qs, k_ref[0, rows, :], v_ref[0, rows, :], m, l, acc, kc <= qc)
    o_ref[0] = _diff_finish(l, acc, _lambda_full(lam_ref), sub_ref[...], tq)


def _attn_prompt(qb, kb, vb, lam_p, subln):
    b, s, d = qb.shape
    tq = 256
    return pl.pallas_call(
        functools.partial(_attn_prompt_kernel, tq=tq),
        grid=(b, N_HEADS, s // tq),
        in_specs=[
            pl.BlockSpec((1, tq, HEAD_W), lambda i, h, j: (i, j, h)),
            pl.BlockSpec((1, s, HEAD_W), lambda i, h, j: (i, 0, h)),
            pl.BlockSpec((1, s, HEAD_W), lambda i, h, j: (i, 0, h)),
            pl.BlockSpec((4, HEAD_DIM), lambda i, h, j: (0, 0)),
            pl.BlockSpec((1, HEAD_W), lambda i, h, j: (0, 0)),
        ],
        out_specs=pl.BlockSpec((1, tq, HEAD_W), lambda i, h, j: (i, j, h)),
        out_shape=jax.ShapeDtypeStruct((b, s, d), BF16),
        compiler_params=pltpu.CompilerParams(
            dimension_semantics=("arbitrary", "arbitrary", "arbitrary"), vmem_limit_bytes=VMEM_LIMIT),
        name="attn_prompt",
    )(qb, kb, vb, lam_p, subln)


def _attn_sample_kernel(q_ref, kn_ref, vn_ref, ck_ref, cv_ref, lam_ref, sub_ref, o_ref,
                        m_ref, l_ref, acc_ref, *, n):
    j = pl.program_id(1)
    last = pl.num_programs(1) - 1

    @pl.when(j == 0)
    def _():
        m_ref[...] = jnp.full(m_ref.shape, NEG_INF, F32)
        l_ref[...] = jnp.zeros(l_ref.shape, F32)
        acc_ref[...] = jnp.zeros(acc_ref.shape, F32)

    for h in range(N_HEADS):
        ln = slice(h * HEAD_W, (h + 1) * HEAD_W)
        qs = _stack_maps(q_ref[0, :, ln])
        m, l, acc = _flash_step(qs, ck_ref[0, :, ln].astype(BF16), cv_ref[0, :, ln].astype(BF16),
                                m_ref[h], l_ref[h], acc_ref[h], None)
        m_ref[h] = m
        l_ref[h] = l
        acc_ref[h] = acc

    @pl.when(j == last)
    def _():
        lam = _lambda_full(lam_ref)
        for h in range(N_HEADS):
            ln = slice(h * HEAD_W, (h + 1) * HEAD_W)
            qs = _stack_maps(q_ref[0, :, ln])
            m, l, acc = _flash_step(qs, kn_ref[0, :, ln], vn_ref[0, :, ln],
                                    m_ref[h], l_ref[h], acc_ref[h], None)
            o_ref[0, :, ln] = _diff_finish(l, acc, lam, sub_ref[...], n)


def _attn_sample(qb, kb, vb, cache_k, cache_v, lam_p, subln):
    b, n, d = qb.shape
    past = cache_k.shape[1]
    tk = min(past, 1024)
    new = pl.BlockSpec((1, n, d), lambda i, j: (i, 0, 0))
    old = pl.BlockSpec((1, tk, d), lambda i, j: (i, j, 0))
    return pl.pallas_call(
        functools.partial(_attn_sample_kernel, n=n),
        grid=(b, past // tk),
        in_specs=[new, new, new, old, old,
                  pl.BlockSpec((4, HEAD_DIM), lambda i, j: (0, 0)),
                  pl.BlockSpec((1, HEAD_W), lambda i, j: (0, 0))],
        out_specs=new,
        out_shape=jax.ShapeDtypeStruct((b, n, d), BF16),
        scratch_shapes=[
            pltpu.VMEM((N_HEADS, 2 * n, 1), F32),
            pltpu.VMEM((N_HEADS, 2 * n, 1), F32),
            pltpu.VMEM((N_HEADS, 2 * n, HEAD_W), F32),
        ],
        compiler_params=pltpu.CompilerParams(
            dimension_semantics=("arbitrary", "arbitrary"), vmem_limit_bytes=VMEM_LIMIT),
        name="attn_sample",
    )(qb, kb, vb, cache_k, cache_v, lam_p, subln)


def _outproj_kernel(x_ref, o_ref, wo_ref, y_ref):
    y_ref[...] = x_ref[...] + jnp.dot(o_ref[...], wo_ref[...], preferred_element_type=F32)


def _attn_outproj(x, ob, wo):
    t, d = x.shape
    tm = min(t, 512)
    row = pl.BlockSpec((tm, d), lambda i: (i, 0))
    return pl.pallas_call(
        _outproj_kernel,
        grid=(t // tm,),
        in_specs=[row, row, pl.BlockSpec((d, d), lambda i: (0, 0))],
        out_specs=row,
        out_shape=jax.ShapeDtypeStruct((t, d), F32),
        compiler_params=pltpu.CompilerParams(
            dimension_semantics=("arbitrary",), vmem_limit_bytes=VMEM_LIMIT),
        name="attn_outproj",
    )(x, ob, wo)


def _trunk(x, conv_state, past_k, past_v, w):
    b, s, d = x.shape
    t = b * s
    x1, new_conv = _conv_layer(x, conv_state, w["g_mix0"], w["conv_w_in"], w["conv_w"], w["conv_w_out"])
    x2 = _peer_layer(x1.reshape(t, d), w["g_ffn0"], w["g_final"], *w["peer0"], final=False)
    k, v, kb, vb, qb = _kvq_proj(x2, w["g_kv"], w["g_mix1"], w["attn_wk"], w["attn_wv"], w["attn_wq"])
    shp = (b, s, d)
    if past_k is None:
        ob = _attn_prompt(qb.reshape(shp), kb.reshape(shp), vb.reshape(shp), w["attn_lambda"], w["attn_subln"])
    else:
        ob = _attn_sample(qb.reshape(shp), kb.reshape(shp), vb.reshape(shp),
                          past_k.reshape(b, -1, d), past_v.reshape(b, -1, d),
                          w["attn_lambda"], w["attn_subln"])
    x3 = _attn_outproj(x2, ob.reshape(t, d), w["attn_wo"])
    y = _peer_layer(x3, w["g_ffn1"], w["g_final"], *w["peer1"], final=True)
    return (y.reshape(shp), k.reshape(b, s, N_HEADS, 2, HEAD_DIM), v.reshape(b, s, N_HEADS, HEAD_W),
            new_conv[None])


def kernel(x_prompt, x_sample, cache_k, cache_v, state_conv, g_mix, g_ffn, conv_w_in, conv_w, conv_w_out, g_kv, attn_wk, attn_wv, attn_wq, attn_lambda, attn_subln, attn_wo, peer_wq, peer_keys, peer_u, peer_v, g_final):
    d = D_MODEL

    def peer_weights(l):
        return (peer_wq[l].T.astype(BF16),
                peer_keys[l].reshape(2 * PEER_HEADS, N_KEYS, N_KEYS).astype(BF16),
                peer_u[l].astype(BF16),
                peer_v[l].T.astype(BF16))

    w = {
        "g_mix0": g_mix[0].reshape(1, d), "g_mix1": g_mix[1].reshape(1, d),
        "g_ffn0": g_ffn[0].reshape(1, d), "g_ffn1": g_ffn[1].reshape(1, d),
        "g_kv": g_kv.reshape(1, d), "g_final": g_final.reshape(1, d),
        "conv_w_in": conv_w_in[0].astype(BF16), "conv_w": conv_w[0], "conv_w_out": conv_w_out[0].astype(BF16),
        "attn_wk": attn_wk.astype(BF16), "attn_wv": attn_wv.astype(BF16), "attn_wq": attn_wq[0].astype(BF16),
        "attn_lambda": attn_lambda[0], "attn_subln": attn_subln[0].reshape(1, HEAD_W),
        "attn_wo": attn_wo[0].astype(BF16),
        "peer0": peer_weights(0), "peer1": peer_weights(1),
    }
    zero_state = jnp.zeros((x_prompt.shape[0], CONV_W - 1, d), F32)
    y_p, k_p, v_p, conv_p = _trunk(x_prompt, zero_state, None, None, w)
    y_s, k_s, v_s, conv_s = _trunk(x_sample, state_conv[0], cache_k, cache_v, w)
    return (y_p, y_s, k_p, v_p, conv_p, k_s, v_s, conv_s)
```

```python
import functools
import math

import jax
import jax.numpy as jnp
from jax import lax
from jax.experimental import pallas as pl
from jax.experimental.pallas import tpu as pltpu

D_MODEL = 1024
CHUNK = 64
CONV_W = 3
HEAD_DIM = 64
N_HEADS = 8
HEAD_W = 2 * HEAD_DIM
PEER_HEADS = 8
N_KEYS = 128
N_EXPERTS = N_KEYS * N_KEYS
PEER_TOPK = 16
PEER_QDIM = 256
EPS = 1e-6
LAMBDA_INIT_L1 = 0.8 - 0.6 * math.exp(-0.3 * 1)

LANES = 128
VMEM_LIMIT = 56 * 1024 * 1024

F32 = jnp.float32
BF16 = jnp.bfloat16
NEG_INF = float("-inf")


def _rms_rows(x, g):
    return x * lax.rsqrt(jnp.mean(x * x, axis=-1, keepdims=True) + EPS) * g


def _conv_kernel(x_ref, st_ref, g_ref, win_ref, cw_ref, wout_ref, o_ref, ns_ref, zp_ref, *, ts):
    s = pl.program_id(1)
    d = D_MODEL

    @pl.when(s == 0)
    def _():
        zp_ref[0:8, :] = jnp.zeros((8, d), F32)
        zp_ref[6:8, :] = st_ref[0]

    x = x_ref[0]
    h = _rms_rows(x, g_ref[...]).astype(BF16)
    p = jnp.dot(h, win_ref[...], preferred_element_type=F32)
    z = p[:, d:2 * d] * p[:, 2 * d:]
    zp_ref[8:8 + ts, :] = z
    conv = (cw_ref[0:1, :] * zp_ref[6:6 + ts, :] + cw_ref[1:2, :] * zp_ref[7:7 + ts, :]
            + cw_ref[2:3, :] * z)
    y = jnp.dot((p[:, :d] * conv).astype(BF16), wout_ref[...], preferred_element_type=F32)
    o_ref[0] = x + y
    ns_ref[0] = zp_ref[ts + 6:ts + 8, :]
    zp_ref[0:8, :] = zp_ref[ts:ts + 8, :]


def _conv_layer(x, state, g, w_in, conv_w, w_out):
    b, s, d = x.shape
    ts = min(s, 512)
    return pl.pallas_call(
        functools.partial(_conv_kernel, ts=ts),
        grid=(b, s // ts),
        in_specs=[
            pl.BlockSpec((1, ts, d), lambda i, j: (i, j, 0)),
            pl.BlockSpec((1, CONV_W - 1, d), lambda i, j: (i, 0, 0)),
            pl.BlockSpec((1, d), lambda i, j: (0, 0)),
            pl.BlockSpec((d, 3 * d), lambda i, j: (0, 0)),
            pl.BlockSpec((CONV_W, d), lambda i, j: (0, 0)),
            pl.BlockSpec((d, d), lambda i, j: (0, 0)),
        ],
        out_specs=[
            pl.BlockSpec((1, ts, d), lambda i, j: (i, j, 0)),
            pl.BlockSpec((1, CONV_W - 1, d), lambda i, j: (i, 0, 0)),
        ],
        out_shape=[
            jax.ShapeDtypeStruct((b, s, d), F32),
            jax.ShapeDtypeStruct((b, CONV_W - 1, d), F32),
        ],
        scratch_shapes=[pltpu.VMEM((ts + 8, d), F32)],
        compiler_params=pltpu.CompilerParams(
            dimension_semantics=("arbitrary", "arbitrary"), vmem_limit_bytes=VMEM_LIMIT),
        name="conv_layer",
    )(x, state, g, w_in, conv_w, w_out)


def _top16(s, want_rank):
    vals = []
    rank = jnp.full(s.shape, float(PEER_TOPK), F32) if want_rank else None
    for a in range(PEER_TOPK):
        m = jnp.max(s, axis=0, keepdims=True)
        hit = s == m
        if want_rank:
            rank = jnp.where(hit, float(a), rank)
        s = jnp.where(hit, NEG_INF, s)
        vals.append(m)
    return vals, rank


def _peer_route(s0, s1):
    v0, _ = _top16(s0, False)
    v1, rank1 = _top16(s1, True)
    v0a = jnp.concatenate(v0, axis=0)
    v1a = jnp.concatenate(v1, axis=0)
    cand = jnp.concatenate(
        [v0[0] + v1a] + [v0[a] + v1a[0:8] for a in range(1, 8)] + [v0a[8:16] + v1[0]], axis=0)
    top = v0[0] + v1[0]
    c = cand
    tau = top
    for _ in range(PEER_TOPK):
        tau = jnp.max(c, axis=0, keepdims=True)
        c = jnp.where(c == tau, NEG_INF, c)
    z = jnp.sum(jnp.where(cand >= tau, jnp.exp(cand - top), 0.0), axis=0, keepdims=True)
    cnt = jnp.zeros(s0.shape, F32)
    for b in range(PEER_TOPK):
        cnt = cnt + jnp.where(s0 + v1[b] >= tau, 1.0, 0.0)
    a_w = jnp.exp(s0 - v0[0]) / z
    b_w = jnp.exp(s1 - v1[0])
    return cnt, rank1, a_w, b_w


def _peer_kernel(x_ref, g_ref, gf_ref, wqt_ref, keys_ref, u_ref, vt_ref, o_ref,
                 ht_ref, sc_ref, rk_ref, bw_ref, cn_ref, aw_ref, acc_ref, a_ref, w_ref,
                 *, tm, ne, final):
    e = pl.program_id(1)
    n_e = pl.num_programs(1)
    rows_per_step = ne // N_KEYS

    @pl.when(e == 0)
    def _prologue():
        h = _rms_rows(x_ref[...], g_ref[...])
        ht_ref[...] = h.T.astype(BF16)
        qt = jnp.dot(wqt_ref[...], ht_ref[...], preferred_element_type=F32).astype(BF16)
        for hc in range(2 * PEER_HEADS):
            sc_ref[hc] = jnp.dot(keys_ref[hc], qt[hc * N_KEYS:(hc + 1) * N_KEYS],
                                 preferred_element_type=F32)
        n_chunks = tm // LANES

        def route_body(idx, carry):
            hd = idx // n_chunks
            ln = pl.ds(pl.multiple_of((idx % n_chunks) * LANES, LANES), LANES)
            cnt, rank1, a_w, b_w = _peer_route(sc_ref[2 * hd, :, ln], sc_ref[2 * hd + 1, :, ln])
            cn_ref[hd, :, ln] = cnt
            aw_ref[hd, :, ln] = a_w
            rk_ref[hd, :, ln] = rank1.astype(BF16)
            bw_ref[hd, :, ln] = b_w.astype(BF16)
            return carry

        lax.fori_loop(0, PEER_HEADS * n_chunks, route_body, 0)
        acc_ref[...] = jnp.zeros(acc_ref.shape, F32)

    unit = 2 * N_KEYS
    n_units = ne // unit
    pack = (N_KEYS // 16, 16, LANES)

    def pre_activations(k):
        rows = slice(k * unit, (k + 1) * unit)
        a_ref[rows, :] = jnp.dot(u_ref[rows, :], ht_ref[...], preferred_element_type=F32)

    def gate_unit(k):
        for r in range(k * 2, k * 2 + 2):
            i0 = pl.ds(pl.multiple_of(e * rows_per_step + (r // 8) * 8, 8), 8)
            rows = slice(r * N_KEYS, (r + 1) * N_KEYS)
            for tc in range(tm // LANES):
                ln = slice(tc * LANES, (tc + 1) * LANES)
                gate = jnp.zeros(pack, BF16)
                for hd in range(PEER_HEADS):
                    c_b = jnp.broadcast_to(cn_ref[hd, i0, ln][r % 8:r % 8 + 1], (16, LANES)).astype(BF16)
                    a_b = jnp.broadcast_to(aw_ref[hd, i0, ln][r % 8:r % 8 + 1], (16, LANES)).astype(BF16)
                    sel = jnp.where(rk_ref[hd, :, ln].reshape(pack) < c_b[None],
                                    bw_ref[hd, :, ln].reshape(pack), jnp.zeros((), BF16))
                    gate = gate + sel * a_b[None]
                pre = a_ref[rows, ln]
                act = 0.5 * pre * (1.0 + lax.erf(pre * (2.0 ** -0.5)))
                w_ref[rows, ln] = gate.reshape(N_KEYS, LANES) * act.astype(BF16)

    def accumulate(k0, k1):
        cols = slice(k0 * unit, k1 * unit)
        acc_ref[...] += jnp.dot(vt_ref[:, cols], w_ref[cols, :], preferred_element_type=F32)

    pre_activations(0)
    for k in range(n_units):
        if k + 1 < n_units:
            pre_activations(k + 1)
        gate_unit(k)
        if k % 2 == 1:
            accumulate(k - 1, k + 1)

    @pl.when(e == n_e - 1)
    def _epilogue():
        y = x_ref[...] + acc_ref[...].T
        if final:
            y = _rms_rows(y, gf_ref[...])
        o_ref[...] = y


def _peer_layer(x, g, g_final, wq_t, keys, u, v_t, *, final):
    t, d = x.shape
    tm = min(t, 512)
    ne = 1024
    return pl.pallas_call(
        functools.partial(_peer_kernel, tm=tm, ne=ne, final=final),
        grid=(t // tm, N_EXPERTS // ne),
        in_specs=[
            pl.BlockSpec((tm, d), lambda i, e: (i, 0)),
            pl.BlockSpec((1, d), lambda i, e: (0, 0)),
            pl.BlockSpec((1, d), lambda i, e: (0, 0)),
            pl.BlockSpec((PEER_HEADS * PEER_QDIM, d), lambda i, e: (0, 0)),
            pl.BlockSpec((2 * PEER_HEADS, N_KEYS, N_KEYS), lambda i, e: (0, 0, 0)),
            pl.BlockSpec((ne, d), lambda i, e: (e, 0)),
            pl.BlockSpec((d, ne), lambda i, e: (0, e)),
        ],
        out_specs=pl.BlockSpec((tm, d), lambda i, e: (i, 0)),
        out_shape=jax.ShapeDtypeStruct((t, d), F32),
        scratch_shapes=[
            pltpu.VMEM((d, tm), BF16),
            pltpu.VMEM((2 * PEER_HEADS, N_KEYS, tm), F32),
            pltpu.VMEM((PEER_HEADS, N_KEYS, tm), BF16),
            pltpu.VMEM((PEER_HEADS, N_KEYS, tm), BF16),
            pltpu.VMEM((PEER_HEADS, N_KEYS, tm), F32),
            pltpu.VMEM((PEER_HEADS, N_KEYS, tm), F32),
            pltpu.VMEM((d, tm), F32),
            pltpu.VMEM((ne, tm), F32),
            pltpu.VMEM((ne, tm), BF16),
        ],
        compiler_params=pltpu.CompilerParams(
            dimension_semantics=("arbitrary", "arbitrary"), vmem_limit_bytes=VMEM_LIMIT),
        name="peer_final" if final else "peer",
    )(x, g, g_final, wq_t, keys, u, v_t)


def _kvq_kernel(x_ref, gkv_ref, gq_ref, wk_ref, wv_ref, wq_ref,
                k_ref, v_ref, kb_ref, vb_ref, qb_ref):
    x = x_ref[...]
    n = x * lax.rsqrt(jnp.mean(x * x, axis=-1, keepdims=True) + EPS)
    hk = (n * gkv_ref[...]).astype(BF16)
    hq = (n * gq_ref[...]).astype(BF16)
    k = jnp.dot(hk, wk_ref[...], preferred_element_type=F32)
    v = jnp.dot(hk, wv_ref[...], preferred_element_type=F32)
    q = jnp.dot(hq, wq_ref[...], preferred_element_type=F32)
    k_ref[...] = k
    v_ref[...] = v
    kb_ref[...] = k.astype(BF16)
    vb_ref[...] = v.astype(BF16)
    qb_ref[...] = (q * (HEAD_DIM ** -0.5)).astype(BF16)


def _kvq_proj(x, g_kv, g_q, wk, wv, wq):
    t, d = x.shape
    tm = min(t, 512)
    row = pl.BlockSpec((tm, d), lambda i: (i, 0))
    vec = pl.BlockSpec((1, d), lambda i: (0, 0))
    mat = pl.BlockSpec((d, d), lambda i: (0, 0))
    return pl.pallas_call(
        _kvq_kernel,
        grid=(t // tm,),
        in_specs=[row, vec, vec, mat, mat, mat],
        out_specs=[row] * 5,
        out_shape=[jax.ShapeDtypeStruct((t, d), F32)] * 2 + [jax.ShapeDtypeStruct((t, d), BF16)] * 3,
        compiler_params=pltpu.CompilerParams(
            dimension_semantics=("arbitrary",), vmem_limit_bytes=VMEM_LIMIT),
        name="kvq_proj",
    )(x, g_kv, g_q, wk, wv, wq)


def _lambda_full(lam_ref):
    lp = lam_ref[...]
    s01 = jnp.sum(lp[0:1] * lp[1:2], axis=-1, keepdims=True)
    s23 = jnp.sum(lp[2:3] * lp[3:4], axis=-1, keepdims=True)
    return jnp.exp(s01) - jnp.exp(s23) + LAMBDA_INIT_L1


def _stack_maps(q):
    lane = lax.broadcasted_iota(jnp.int32, q.shape, 1)
    zero = jnp.zeros((), q.dtype)
    return jnp.concatenate([jnp.where(lane < HEAD_DIM, q, zero), jnp.where(lane >= HEAD_DIM, q, zero)],
                           axis=0)


def _flash_step(qs, k, v, m, l, acc, mask):
    s = lax.dot_general(qs, k, (((1,), (1,)), ((), ())), preferred_element_type=F32)
    if mask is not None:
        s = jnp.where(mask, s, NEG_INF)
    m_new = jnp.maximum(m, jnp.max(s, axis=-1, keepdims=True))
    alpha = jnp.exp(m - m_new)
    p = jnp.exp(s - m_new)
    l = alpha * l + jnp.sum(p, axis=-1, keepdims=True)
    acc = alpha * acc + jnp.dot(p.astype(BF16), v, preferred_element_type=F32)
    return m_new, l, acc


def _diff_finish(l, acc, lam, subln, n):
    o = acc / l
    o = o[:n] - lam * o[n:]
    o = o * lax.rsqrt(jnp.mean(o * o, axis=-1, keepdims=True) + EPS) * subln
    return (o * (1.0 - LAMBDA_INIT_L1)).astype(BF16)


def _attn_prompt_kernel(q_ref, k_ref, v_ref, lam_ref, sub_ref, o_ref,
                        vt_ref, qt_ref, sa_ref, sb_ref, acc_ref, *, tq, tk):
    qi = pl.program_id(2)
    n2 = 2 * tq
    seq = v_ref.shape[1]

    @pl.when(qi == 0)
    def _():
        for c in range(seq // 512):
            rows = slice(c * 512, (c + 1) * 512)
            vt_ref[:, rows] = v_ref[0, rows, :].astype(F32).T.astype(BF16)

    qt_ref[...] = _stack_maps(q_ref[0]).astype(F32).T.astype(BF16)
    acc_ref[...] = jnp.zeros(acc_ref.shape, F32)

    def scores(t):
        rows = pl.ds(pl.multiple_of(t * tk, tk), tk)
        return jnp.dot(k_ref[0, rows, :], qt_ref[...], preferred_element_type=F32)

    def consume(s, t, m, l):
        cols = pl.ds(pl.multiple_of(t * tk, tk), tk)
        m_new = jnp.maximum(m, jnp.max(s, axis=0, keepdims=True))
        alpha = jnp.exp(m - m_new)
        p = jnp.exp(s - m_new)
        l = alpha * l + jnp.sum(p, axis=0, keepdims=True)
        acc_ref[...] = alpha * acc_ref[...] + jnp.dot(vt_ref[:, cols], p.astype(BF16),
                                                      preferred_element_type=F32)
        return m_new, l

    def pair(p, carry):
        m, l = carry
        sb_ref[...] = scores(2 * p + 1)
        m, l = consume(sa_ref[...], 2 * p, m, l)
        sa_ref[...] = scores(2 * p + 2)
        return consume(sb_ref[...], 2 * p + 1, m, l)

    def diag_bias(s, half):
        qc = (lax.broadcasted_iota(jnp.int32, (1, n2), 1) % tq) // CHUNK
        parts = []
        for c in range(tk // CHUNK):
            kc = half * (tk // CHUNK) + c
            bias = jnp.where(qc >= kc, 0.0, NEG_INF)
            parts.append(s[c * CHUNK:(c + 1) * CHUNK] + bias)
        return jnp.concatenate(parts, axis=0)

    sa_ref[...] = scores(0)
    m0 = jnp.full((1, n2), NEG_INF, F32)
    l0 = jnp.zeros((1, n2), F32)
    m, l = lax.fori_loop(0, qi, pair, (m0, l0))
    t0 = qi * (tq // tk)
    sb_ref[...] = scores(t0 + 1)
    m, l = consume(diag_bias(sa_ref[...], 0), t0, m, l)
    m, l = consume(diag_bias(sb_ref[...], 1), t0 + 1, m, l)

    o = acc_ref[...] * (1.0 / l)
    o = o[:, :tq] - _lambda_full(lam_ref) * o[:, tq:]
    o = o * lax.rsqrt(jnp.mean(o * o, axis=0, keepdims=True) + EPS)
    o_ref[0] = (o.T * sub_ref[...] * (1.0 - LAMBDA_INIT_L1)).astype(BF16)


def _attn_prompt(qb, kb, vb, lam_p, subln):
    b, s, d = qb.shape
    tq = 512
    tk = tq // 2
    return pl.pallas_call(
        functools.partial(_attn_prompt_kernel, tq=tq, tk=tk),
        grid=(b, N_HEADS, s // tq),
        in_specs=[
            pl.BlockSpec((1, tq, HEAD_W), lambda i, h, j: (i, j, h)),
            pl.BlockSpec((1, s, HEAD_W), lambda i, h, j: (i, 0, h)),
            pl.BlockSpec((1, s, HEAD_W), lambda i, h, j: (i, 0, h)),
            pl.BlockSpec((4, HEAD_DIM), lambda i, h, j: (0, 0)),
            pl.BlockSpec((1, HEAD_W), lambda i, h, j: (0, 0)),
        ],
        out_specs=pl.BlockSpec((1, tq, HEAD_W), lambda i, h, j: (i, j, h)),
        out_shape=jax.ShapeDtypeStruct((b, s, d), BF16),
        scratch_shapes=[
            pltpu.VMEM((HEAD_W, s), BF16),
            pltpu.VMEM((HEAD_W, 2 * tq), BF16),
            pltpu.VMEM((tk, 2 * tq), F32),
            pltpu.VMEM((tk, 2 * tq), F32),
            pltpu.VMEM((HEAD_W, 2 * tq), F32),
        ],
        compiler_params=pltpu.CompilerParams(
            dimension_semantics=("arbitrary", "arbitrary", "arbitrary"), vmem_limit_bytes=VMEM_LIMIT),
        name="attn_prompt",
    )(qb, kb, vb, lam_p, subln)


def _attn_sample_kernel(q_ref, kn_ref, vn_ref, ck_ref, cv_ref, lam_ref, sub_ref, o_ref,
                        m_ref, l_ref, acc_ref, *, n):
    j = pl.program_id(1)
    last = pl.num_programs(1) - 1

    @pl.when(j == 0)
    def _():
        m_ref[...] = jnp.full(m_ref.shape, NEG_INF, F32)
        l_ref[...] = jnp.zeros(l_ref.shape, F32)
        acc_ref[...] = jnp.zeros(acc_ref.shape, F32)

    for h in range(N_HEADS):
        ln = slice(h * HEAD_W, (h + 1) * HEAD_W)
        qs = _stack_maps(q_ref[0, :, ln])
        m, l, acc = _flash_step(qs, ck_ref[0, :, ln].astype(BF16), cv_ref[0, :, ln].astype(BF16),
                                m_ref[h], l_ref[h], acc_ref[h], None)
        m_ref[h] = m
        l_ref[h] = l
        acc_ref[h] = acc

    @pl.when(j == last)
    def _():
        lam = _lambda_full(lam_ref)
        for h in range(N_HEADS):
            ln = slice(h * HEAD_W, (h + 1) * HEAD_W)
            qs = _stack_maps(q_ref[0, :, ln])
            m, l, acc = _flash_step(qs, kn_ref[0, :, ln], vn_ref[0, :, ln],
                                    m_ref[h], l_ref[h], acc_ref[h], None)
            o_ref[0, :, ln] = _diff_finish(l, acc, lam, sub_ref[...], n)


def _attn_sample(qb, kb, vb, cache_k, cache_v, lam_p, subln):
    b, n, d = qb.shape
    past = cache_k.shape[1]
    tk = min(past, 1024)
    new = pl.BlockSpec((1, n, d), lambda i, j: (i, 0, 0))
    old = pl.BlockSpec((1, tk, d), lambda i, j: (i, j, 0))
    return pl.pallas_call(
        functools.partial(_attn_sample_kernel, n=n),
        grid=(b, past // tk),
        in_specs=[new, new, new, old, old,
                  pl.BlockSpec((4, HEAD_DIM), lambda i, j: (0, 0)),
                  pl.BlockSpec((1, HEAD_W), lambda i, j: (0, 0))],
        out_specs=new,
        out_shape=jax.ShapeDtypeStruct((b, n, d), BF16),
        scratch_shapes=[
            pltpu.VMEM((N_HEADS, 2 * n, 1), F32),
            pltpu.VMEM((N_HEADS, 2 * n, 1), F32),
            pltpu.VMEM((N_HEADS, 2 * n, HEAD_W), F32),
        ],
        compiler_params=pltpu.CompilerParams(
            dimension_semantics=("arbitrary", "arbitrary"), vmem_limit_bytes=VMEM_LIMIT),
        name="attn_sample",
    )(qb, kb, vb, cache_k, cache_v, lam_p, subln)


def _outproj_kernel(x_ref, o_ref, wo_ref, y_ref):
    y_ref[...] = x_ref[...] + jnp.dot(o_ref[...], wo_ref[...], preferred_element_type=F32)


def _attn_outproj(x, ob, wo):
    t, d = x.shape
    tm = min(t, 512)
    row = pl.BlockSpec((tm, d), lambda i: (i, 0))
    return pl.pallas_call(
        _outproj_kernel,
        grid=(t // tm,),
        in_specs=[row, row, pl.BlockSpec((d, d), lambda i: (0, 0))],
        out_specs=row,
        out_shape=jax.ShapeDtypeStruct((t, d), F32),
        compiler_params=pltpu.CompilerParams(
            dimension_semantics=("arbitrary",), vmem_limit_bytes=VMEM_LIMIT),
        name="attn_outproj",
    )(x, ob, wo)


def _trunk(x, conv_state, past_k, past_v, w):
    b, s, d = x.shape
    t = b * s
    x1, new_conv = _conv_layer(x, conv_state, w["g_mix0"], w["conv_w_in"], w["conv_w"], w["conv_w_out"])
    x2 = _peer_layer(x1.reshape(t, d), w["g_ffn0"], w["g_final"], *w["peer0"], final=False)
    k, v, kb, vb, qb = _kvq_proj(x2, w["g_kv"], w["g_mix1"], w["attn_wk"], w["attn_wv"], w["attn_wq"])
    shp = (b, s, d)
    if past_k is None:
        ob = _attn_prompt(qb.reshape(shp), kb.reshape(shp), vb.reshape(shp), w["attn_lambda"], w["attn_subln"])
    else:
        ob = _attn_sample(qb.reshape(shp), kb.reshape(shp), vb.reshape(shp),
                          past_k.reshape(b, -1, d), past_v.reshape(b, -1, d),
                          w["attn_lambda"], w["attn_subln"])
    x3 = _attn_outproj(x2, ob.reshape(t, d), w["attn_wo"])
    y = _peer_layer(x3, w["g_ffn1"], w["g_final"], *w["peer1"], final=True)
    return (y.reshape(shp), k.reshape(b, s, N_HEADS, 2, HEAD_DIM), v.reshape(b, s, N_HEADS, HEAD_W),
            new_conv[None])


def kernel(x_prompt, x_sample, cache_k, cache_v, state_conv, g_mix, g_ffn, conv_w_in, conv_w, conv_w_out, g_kv, attn_wk, attn_wv, attn_wq, attn_lambda, attn_subln, attn_wo, peer_wq, peer_keys, peer_u, peer_v, g_final):
    d = D_MODEL

    def peer_weights(l):
        return (peer_wq[l].T.astype(BF16),
                peer_keys[l].reshape(2 * PEER_HEADS, N_KEYS, N_KEYS).astype(BF16),
                peer_u[l].astype(BF16),
                peer_v[l].T.astype(BF16))

    w = {
        "g_mix0": g_mix[0].reshape(1, d), "g_mix1": g_mix[1].reshape(1, d),
        "g_ffn0": g_ffn[0].reshape(1, d), "g_ffn1": g_ffn[1].reshape(1, d),
        "g_kv": g_kv.reshape(1, d), "g_final": g_final.reshape(1, d),
        "conv_w_in": conv_w_in[0].astype(BF16), "conv_w": conv_w[0], "conv_w_out": conv_w_out[0].astype(BF16),
        "attn_wk": attn_wk.astype(BF16), "attn_wv": attn_wv.astype(BF16), "attn_wq": attn_wq[0].astype(BF16),
        "attn_lambda": attn_lambda[0], "attn_subln": attn_subln[0].reshape(1, HEAD_W),
        "attn_wo": attn_wo[0].astype(BF16),
        "peer0": peer_weights(0), "peer1": peer_weights(1),
    }
    zero_state = jnp.zeros((x_prompt.shape[0], CONV_W - 1, d), F32)
    y_p, k_p, v_p, conv_p = _trunk(x_prompt, zero_state, None, None, w)
    y_s, k_s, v_s, conv_s = _trunk(x_sample, state_conv[0], cache_k, cache_v, w)
    return (y_p, y_s, k_p, v_p, conv_p, k_s, v_s, conv_s)
```

```python
import functools
import math

import jax
import jax.numpy as jnp
from jax import lax
from jax.experimental import pallas as pl
from jax.experimental.pallas import tpu as pltpu

D_MODEL = 1024
CHUNK = 64
CONV_W = 3
HEAD_DIM = 64
N_HEADS = 8
HEAD_W = 2 * HEAD_DIM
PEER_HEADS = 8
N_KEYS = 128
N_EXPERTS = N_KEYS * N_KEYS
PEER_TOPK = 16
PEER_QDIM = 256
PEER_UNIT = 2 * N_KEYS
PEER_VBLOCK = 4 * N_KEYS
EPS = 1e-6
LAMBDA_INIT_L1 = 0.8 - 0.6 * math.exp(-0.3 * 1)

LANES = 128
VMEM_LIMIT = 56 * 1024 * 1024

F32 = jnp.float32
BF16 = jnp.bfloat16
NEG_INF = float("-inf")


def _rms_rows(x, g):
    return x * lax.rsqrt(jnp.mean(x * x, axis=-1, keepdims=True) + EPS) * g


def _conv_kernel(x_ref, st_ref, g_ref, win_ref, cw_ref, wout_ref, o_ref, ns_ref, zp_ref, *, ts):
    s = pl.program_id(1)
    d = D_MODEL

    @pl.when(s == 0)
    def _():
        zp_ref[0:8, :] = jnp.zeros((8, d), F32)
        zp_ref[6:8, :] = st_ref[0]

    x = x_ref[0]
    h = _rms_rows(x, g_ref[...]).astype(BF16)
    p = jnp.dot(h, win_ref[...], preferred_element_type=F32)
    z = p[:, d:2 * d] * p[:, 2 * d:]
    zp_ref[8:8 + ts, :] = z
    conv = (cw_ref[0:1, :] * zp_ref[6:6 + ts, :] + cw_ref[1:2, :] * zp_ref[7:7 + ts, :]
            + cw_ref[2:3, :] * z)
    y = jnp.dot((p[:, :d] * conv).astype(BF16), wout_ref[...], preferred_element_type=F32)
    o_ref[0] = x + y
    ns_ref[0] = zp_ref[ts + 6:ts + 8, :]
    zp_ref[0:8, :] = zp_ref[ts:ts + 8, :]


def _conv_layer(x, state, g, w_in, conv_w, w_out):
    b, s, d = x.shape
    ts = min(s, 512)
    return pl.pallas_call(
        functools.partial(_conv_kernel, ts=ts),
        grid=(b, s // ts),
        in_specs=[
            pl.BlockSpec((1, ts, d), lambda i, j: (i, j, 0)),
            pl.BlockSpec((1, CONV_W - 1, d), lambda i, j: (i, 0, 0)),
            pl.BlockSpec((1, d), lambda i, j: (0, 0)),
            pl.BlockSpec((d, 3 * d), lambda i, j: (0, 0)),
            pl.BlockSpec((CONV_W, d), lambda i, j: (0, 0)),
            pl.BlockSpec((d, d), lambda i, j: (0, 0)),
        ],
        out_specs=[
            pl.BlockSpec((1, ts, d), lambda i, j: (i, j, 0)),
            pl.BlockSpec((1, CONV_W - 1, d), lambda i, j: (i, 0, 0)),
        ],
        out_shape=[
            jax.ShapeDtypeStruct((b, s, d), F32),
            jax.ShapeDtypeStruct((b, CONV_W - 1, d), F32),
        ],
        scratch_shapes=[pltpu.VMEM((ts + 8, d), F32)],
        compiler_params=pltpu.CompilerParams(
            dimension_semantics=("arbitrary", "arbitrary"), vmem_limit_bytes=VMEM_LIMIT),
        name="conv_layer",
    )(x, state, g, w_in, conv_w, w_out)


def _top16(s, want_rank):
    vals = []
    rank = jnp.full(s.shape, float(PEER_TOPK), F32) if want_rank else None
    for a in range(PEER_TOPK):
        m = jnp.max(s, axis=0, keepdims=True)
        hit = s == m
        if want_rank:
            rank = jnp.where(hit, float(a), rank)
        s = jnp.where(hit, NEG_INF, s)
        vals.append(m)
    return vals, rank


def _peer_route(s0, s1):
    v0, _ = _top16(s0, False)
    v1, rank1 = _top16(s1, True)
    v0a = jnp.concatenate(v0, axis=0)
    v1a = jnp.concatenate(v1, axis=0)
    cand = jnp.concatenate(
        [v0[0] + v1a] + [v0[a] + v1a[0:8] for a in range(1, 8)] + [v0a[8:16] + v1[0]], axis=0)
    top = v0[0] + v1[0]
    c = cand
    tau = top
    for _ in range(PEER_TOPK):
        tau = jnp.max(c, axis=0, keepdims=True)
        c = jnp.where(c == tau, NEG_INF, c)
    z = jnp.sum(jnp.where(cand >= tau, jnp.exp(cand - top), 0.0), axis=0, keepdims=True)
    cnt = jnp.zeros(s0.shape, F32)
    low = 4
    for b in range(low):
        cnt = cnt + jnp.where(s0 + v1[b] >= tau, 1.0, 0.0)
    for a in range(PEER_TOPK // (low + 1)):
        hi = jnp.zeros(tau.shape, F32)
        for b in range(low, PEER_TOPK // (a + 1)):
            hi = hi + jnp.where(v0[a] + v1[b] >= tau, 1.0, 0.0)
        cnt = cnt + jnp.where(s0 == v0[a], hi, 0.0)
    a_w = jnp.exp(s0 - v0[0]) / z
    b_w = jnp.exp(s1 - v1[0])
    return cnt, rank1, a_w, b_w


def _peer_kernel(x_ref, g_ref, gf_ref, wqt_ref, keys_ref, *refs, tm, ne, final):
    n_u = ne // N_KEYS
    n_v = ne // PEER_VBLOCK
    u_refs = refs[:n_u]
    vt_refs = refs[n_u:n_u + n_v]
    (o_ref, ht_ref, sc_ref, rk_ref, bw_ref, cn_ref, aw_ref, acc_ref, a_ref, w_ref) = refs[n_u + n_v:]
    n_units = ne // PEER_UNIT
    e = pl.program_id(1)
    n_e = pl.num_programs(1)
    rows_per_step = ne // N_KEYS

    @pl.when(e == 0)
    def _prologue():
        h = _rms_rows(x_ref[...], g_ref[...])
        ht_ref[...] = h.T.astype(BF16)
        qt = jnp.dot(wqt_ref[...], ht_ref[...], preferred_element_type=F32).astype(BF16)
        for hc in range(2 * PEER_HEADS):
            sc_ref[hc] = jnp.dot(keys_ref[hc], qt[hc * N_KEYS:(hc + 1) * N_KEYS],
                                 preferred_element_type=F32)
        n_chunks = tm // LANES

        def route_body(idx, carry):
            hd = idx // n_chunks
            ln = pl.ds(pl.multiple_of((idx % n_chunks) * LANES, LANES), LANES)
            cnt, rank1, a_w, b_w = _peer_route(sc_ref[2 * hd, :, ln], sc_ref[2 * hd + 1, :, ln])
            cn_ref[hd, :, ln] = cnt
            aw_ref[hd, :, ln] = a_w
            rk_ref[hd, :, ln] = rank1.astype(BF16)
            bw_ref[hd, :, ln] = b_w.astype(BF16)
            return carry

        lax.fori_loop(0, PEER_HEADS * n_chunks, route_body, 0)
        acc_ref[...] = jnp.zeros(acc_ref.shape, F32)

    unit = PEER_UNIT
    pack = (N_KEYS // 16, 16, LANES)

    def pre_activations(k):
        for r in range(k * 2, k * 2 + 2):
            rows = slice(r * N_KEYS, (r + 1) * N_KEYS)
            a_ref[rows, :] = jnp.dot(u_refs[r][...], ht_ref[...], preferred_element_type=F32)

    def gate_unit(k):
        for r in range(k * 2, k * 2 + 2):
            i0 = pl.ds(pl.multiple_of(e * rows_per_step + (r // 8) * 8, 8), 8)
            rows = slice(r * N_KEYS, (r + 1) * N_KEYS)
            for tc in range(tm // LANES):
                ln = slice(tc * LANES, (tc + 1) * LANES)
                gate = jnp.zeros(pack, BF16)
                for hd in range(PEER_HEADS):
                    c_b = jnp.broadcast_to(cn_ref[hd, i0, ln][r % 8:r % 8 + 1], (16, LANES)).astype(BF16)
                    a_b = jnp.broadcast_to(aw_ref[hd, i0, ln][r % 8:r % 8 + 1], (16, LANES)).astype(BF16)
                    sel = jnp.where(rk_ref[hd, :, ln].reshape(pack) < c_b[None],
                                    bw_ref[hd, :, ln].reshape(pack), jnp.zeros((), BF16))
                    gate = gate + sel * a_b[None]
                pre = a_ref[rows, ln]
                act = 0.5 * pre * (1.0 + lax.erf(pre * (2.0 ** -0.5)))
                w_ref[rows, ln] = gate.reshape(N_KEYS, LANES) * act.astype(BF16)

    def accumulate(blk):
        rows = slice(blk * PEER_VBLOCK, (blk + 1) * PEER_VBLOCK)
        acc_ref[...] += jnp.dot(vt_refs[blk][0], w_ref[rows, :], preferred_element_type=F32)

    units_per_vblock = PEER_VBLOCK // unit
    pre_activations(0)
    for k in range(n_units):
        if k + 1 < n_units:
            pre_activations(k + 1)
        gate_unit(k)
        if (k + 1) % units_per_vblock == 0:
            accumulate(k // units_per_vblock)

    @pl.when(e == n_e - 1)
    def _epilogue():
        y = x_ref[...] + acc_ref[...].T
        if final:
            y = _rms_rows(y, gf_ref[...])
        o_ref[...] = y


def _peer_layer(x, g, g_final, wq_t, keys, u, v_t, *, final):
    t, d = x.shape
    tm = min(t, 512)
    ne = 1024
    n_u = ne // N_KEYS
    n_v = ne // PEER_VBLOCK
    return pl.pallas_call(
        functools.partial(_peer_kernel, tm=tm, ne=ne, final=final),
        grid=(t // tm, N_EXPERTS // ne),
        in_specs=[
            pl.BlockSpec((tm, d), lambda i, e: (i, 0)),
            pl.BlockSpec((1, d), lambda i, e: (0, 0)),
            pl.BlockSpec((1, d), lambda i, e: (0, 0)),
            pl.BlockSpec((PEER_HEADS * PEER_QDIM, d), lambda i, e: (0, 0)),
            pl.BlockSpec((2 * PEER_HEADS, N_KEYS, N_KEYS), lambda i, e: (0, 0, 0)),
        ] + [
            pl.BlockSpec((N_KEYS, d), functools.partial(lambda i, e, r: (e * n_u + r, 0), r=r))
            for r in range(n_u)
        ] + [
            pl.BlockSpec((1, d, PEER_VBLOCK), functools.partial(lambda i, e, b: (e * n_v + b, 0, 0), b=b))
            for b in range(n_v)
        ],
        out_specs=pl.BlockSpec((tm, d), lambda i, e: (i, 0)),
        out_shape=jax.ShapeDtypeStruct((t, d), F32),
        scratch_shapes=[
            pltpu.VMEM((d, tm), BF16),
            pltpu.VMEM((2 * PEER_HEADS, N_KEYS, tm), F32),
            pltpu.VMEM((PEER_HEADS, N_KEYS, tm), BF16),
            pltpu.VMEM((PEER_HEADS, N_KEYS, tm), BF16),
            pltpu.VMEM((PEER_HEADS, N_KEYS, tm), F32),
            pltpu.VMEM((PEER_HEADS, N_KEYS, tm), F32),
            pltpu.VMEM((d, tm), F32),
            pltpu.VMEM((ne, tm), F32),
            pltpu.VMEM((ne, tm), BF16),
        ],
        compiler_params=pltpu.CompilerParams(
            dimension_semantics=("arbitrary", "arbitrary"), vmem_limit_bytes=VMEM_LIMIT),
        name="peer_final" if final else "peer",
    )(x, g, g_final, wq_t, keys, *([u] * n_u), *([v_t] * n_v))


def _kvq_kernel(x_ref, gkv_ref, gq_ref, wk_ref, wv_ref, wq_ref,
                k_ref, v_ref, kb_ref, vb_ref, qb_ref):
    x = x_ref[...]
    n = x * lax.rsqrt(jnp.mean(x * x, axis=-1, keepdims=True) + EPS)
    hk = (n * gkv_ref[...]).astype(BF16)
    hq = (n * gq_ref[...]).astype(BF16)
    k = jnp.dot(hk, wk_ref[...], preferred_element_type=F32)
    v = jnp.dot(hk, wv_ref[...], preferred_element_type=F32)
    q = jnp.dot(hq, wq_ref[...], preferred_element_type=F32)
    k_ref[...] = k
    v_ref[...] = v
    kb_ref[...] = k.astype(BF16)
    vb_ref[...] = v.astype(BF16)
    qb_ref[...] = (q * (HEAD_DIM ** -0.5)).astype(BF16)


def _kvq_proj(x, g_kv, g_q, wk, wv, wq):
    t, d = x.shape
    tm = min(t, 512)
    row = pl.BlockSpec((tm, d), lambda i: (i, 0))
    vec = pl.BlockSpec((1, d), lambda i: (0, 0))
    mat = pl.BlockSpec((d, d), lambda i: (0, 0))
    return pl.pallas_call(
        _kvq_kernel,
        grid=(t // tm,),
        in_specs=[row, vec, vec, mat, mat, mat],
        out_specs=[row] * 5,
        out_shape=[jax.ShapeDtypeStruct((t, d), F32)] * 2 + [jax.ShapeDtypeStruct((t, d), BF16)] * 3,
        compiler_params=pltpu.CompilerParams(
            dimension_semantics=("arbitrary",), vmem_limit_bytes=VMEM_LIMIT),
        name="kvq_proj",
    )(x, g_kv, g_q, wk, wv, wq)


def _lambda_full(lam_ref):
    lp = lam_ref[...]
    s01 = jnp.sum(lp[0:1] * lp[1:2], axis=-1, keepdims=True)
    s23 = jnp.sum(lp[2:3] * lp[3:4], axis=-1, keepdims=True)
    return jnp.exp(s01) - jnp.exp(s23) + LAMBDA_INIT_L1


def _stack_maps(q):
    lane = lax.broadcasted_iota(jnp.int32, q.shape, 1)
    zero = jnp.zeros((), q.dtype)
    return jnp.concatenate([jnp.where(lane < HEAD_DIM, q, zero), jnp.where(lane >= HEAD_DIM, q, zero)],
                           axis=0)


def _flash_step(qs, k, v, m, l, acc, mask):
    s = lax.dot_general(qs, k, (((1,), (1,)), ((), ())), preferred_element_type=F32)
    if mask is not None:
        s = jnp.where(mask, s, NEG_INF)
    m_new = jnp.maximum(m, jnp.max(s, axis=-1, keepdims=True))
    alpha = jnp.exp(m - m_new)
    p = jnp.exp(s - m_new)
    l = alpha * l + jnp.sum(p, axis=-1, keepdims=True)
    acc = alpha * acc + jnp.dot(p.astype(BF16), v, preferred_element_type=F32)
    return m_new, l, acc


def _diff_finish(l, acc, lam, subln, n):
    o = acc / l
    o = o[:n] - lam * o[n:]
    o = o * lax.rsqrt(jnp.mean(o * o, axis=-1, keepdims=True) + EPS) * subln
    return (o * (1.0 - LAMBDA_INIT_L1)).astype(BF16)


def _attn_prompt_kernel(q_ref, k_ref, v_ref, lam_ref, sub_ref, o_ref,
                        vt_ref, qt_ref, sa_ref, sb_ref, acc_ref, *, tq, tk):
    qi = pl.program_id(2)
    n2 = 2 * tq
    seq = v_ref.shape[1]

    @pl.when(qi == 0)
    def _():
        for c in range(seq // 512):
            rows = slice(c * 512, (c + 1) * 512)
            vt_ref[:, rows] = v_ref[0, rows, :].astype(F32).T.astype(BF16)

    qt_ref[...] = _stack_maps(q_ref[0]).astype(F32).T.astype(BF16)
    acc_ref[...] = jnp.zeros(acc_ref.shape, F32)

    def scores(t):
        rows = pl.ds(pl.multiple_of(t * tk, tk), tk)
        return jnp.dot(k_ref[0, rows, :], qt_ref[...], preferred_element_type=F32)

    def consume(s, t, m, l):
        cols = pl.ds(pl.multiple_of(t * tk, tk), tk)
        m_new = jnp.maximum(m, jnp.max(s, axis=0, keepdims=True))
        alpha = jnp.exp(m - m_new)
        p = jnp.exp(s - m_new)
        l = alpha * l + jnp.sum(p, axis=0, keepdims=True)
        acc_ref[...] = alpha * acc_ref[...] + jnp.dot(vt_ref[:, cols], p.astype(BF16),
                                                      preferred_element_type=F32)
        return m_new, l

    def pair(p, carry):
        m, l = carry
        sb_ref[...] = scores(2 * p + 1)
        m, l = consume(sa_ref[...], 2 * p, m, l)
        sa_ref[...] = scores(2 * p + 2)
        return consume(sb_ref[...], 2 * p + 1, m, l)

    def diag_bias(s, half):
        qc = (lax.broadcasted_iota(jnp.int32, (1, n2), 1) % tq) // CHUNK
        parts = []
        for c in range(tk // CHUNK):
            kc = half * (tk // CHUNK) + c
            bias = jnp.where(qc >= kc, 0.0, NEG_INF)
            parts.append(s[c * CHUNK:(c + 1) * CHUNK] + bias)
        return jnp.concatenate(parts, axis=0)

    sa_ref[...] = scores(0)
    m0 = jnp.full((1, n2), NEG_INF, F32)
    l0 = jnp.zeros((1, n2), F32)
    m, l = lax.fori_loop(0, qi, pair, (m0, l0))
    t0 = qi * (tq // tk)
    sb_ref[...] = scores(t0 + 1)
    m, l = consume(diag_bias(sa_ref[...], 0), t0, m, l)
    m, l = consume(diag_bias(sb_ref[...], 1), t0 + 1, m, l)

    o = acc_ref[...] * (1.0 / l)
    o = o[:, :tq] - _lambda_full(lam_ref) * o[:, tq:]
    o = o * lax.rsqrt(jnp.mean(o * o, axis=0, keepdims=True) + EPS)
    o_ref[0] = (o.T * sub_ref[...] * (1.0 - LAMBDA_INIT_L1)).astype(BF16)


def _attn_prompt(qb, kb, vb, lam_p, subln):
    b, s, d = qb.shape
    tq = 512
    tk = tq // 2
    return pl.pallas_call(
        functools.partial(_attn_prompt_kernel, tq=tq, tk=tk),
        grid=(b, N_HEADS, s // tq),
        in_specs=[
            pl.BlockSpec((1, tq, HEAD_W), lambda i, h, j: (i, j, h)),
            pl.BlockSpec((1, s, HEAD_W), lambda i, h, j: (i, 0, h)),
            pl.BlockSpec((1, s, HEAD_W), lambda i, h, j: (i, 0, h)),
            pl.BlockSpec((4, HEAD_DIM), lambda i, h, j: (0, 0)),
            pl.BlockSpec((1, HEAD_W), lambda i, h, j: (0, 0)),
        ],
        out_specs=pl.BlockSpec((1, tq, HEAD_W), lambda i, h, j: (i, j, h)),
        out_shape=jax.ShapeDtypeStruct((b, s, d), BF16),
        scratch_shapes=[
            pltpu.VMEM((HEAD_W, s), BF16),
            pltpu.VMEM((HEAD_W, 2 * tq), BF16),
            pltpu.VMEM((tk, 2 * tq), F32),
            pltpu.VMEM((tk, 2 * tq), F32),
            pltpu.VMEM((HEAD_W, 2 * tq), F32),
        ],
        compiler_params=pltpu.CompilerParams(
            dimension_semantics=("arbitrary", "arbitrary", "arbitrary"), vmem_limit_bytes=VMEM_LIMIT),
        name="attn_prompt",
    )(qb, kb, vb, lam_p, subln)


def _attn_sample_kernel(q_ref, kn_ref, vn_ref, ck_ref, cv_ref, lam_ref, sub_ref, o_ref,
                        m_ref, l_ref, acc_ref, *, n):
    j = pl.program_id(1)
    last = pl.num_programs(1) - 1

    @pl.when(j == 0)
    def _():
        m_ref[...] = jnp.full(m_ref.shape, NEG_INF, F32)
        l_ref[...] = jnp.zeros(l_ref.shape, F32)
        acc_ref[...] = jnp.zeros(acc_ref.shape, F32)

    for h in range(N_HEADS):
        ln = slice(h * HEAD_W, (h + 1) * HEAD_W)
        qs = _stack_maps(q_ref[0, :, ln])
        m, l, acc = _flash_step(qs, ck_ref[0, :, ln].astype(BF16), cv_ref[0, :, ln].astype(BF16),
                                m_ref[h], l_ref[h], acc_ref[h], None)
        m_ref[h] = m
        l_ref[h] = l
        acc_ref[h] = acc

    @pl.when(j == last)
    def _():
        lam = _lambda_full(lam_ref)
        for h in range(N_HEADS):
            ln = slice(h * HEAD_W, (h + 1) * HEAD_W)
            qs = _stack_maps(q_ref[0, :, ln])
            m, l, acc = _flash_step(qs, kn_ref[0, :, ln], vn_ref[0, :, ln],
                                    m_ref[h], l_ref[h], acc_ref[h], None)
            o_ref[0, :, ln] = _diff_finish(l, acc, lam, sub_ref[...], n)


def _attn_sample(qb, kb, vb, cache_k, cache_v, lam_p, subln):
    b, n, d = qb.shape
    past = cache_k.shape[1]
    tk = min(past, 1024)
    new = pl.BlockSpec((1, n, d), lambda i, j: (i, 0, 0))
    old = pl.BlockSpec((1, tk, d), lambda i, j: (i, j, 0))
    return pl.pallas_call(
        functools.partial(_attn_sample_kernel, n=n),
        grid=(b, past // tk),
        in_specs=[new, new, new, old, old,
                  pl.BlockSpec((4, HEAD_DIM), lambda i, j: (0, 0)),
                  pl.BlockSpec((1, HEAD_W), lambda i, j: (0, 0))],
        out_specs=new,
        out_shape=jax.ShapeDtypeStruct((b, n, d), BF16),
        scratch_shapes=[
            pltpu.VMEM((N_HEADS, 2 * n, 1), F32),
            pltpu.VMEM((N_HEADS, 2 * n, 1), F32),
            pltpu.VMEM((N_HEADS, 2 * n, HEAD_W), F32),
        ],
        compiler_params=pltpu.CompilerParams(
            dimension_semantics=("arbitrary", "arbitrary"), vmem_limit_bytes=VMEM_LIMIT),
        name="attn_sample",
    )(qb, kb, vb, cache_k, cache_v, lam_p, subln)


def _outproj_kernel(x_ref, o_ref, wo_ref, y_ref):
    y_ref[...] = x_ref[...] + jnp.dot(o_ref[...], wo_ref[...], preferred_element_type=F32)


def _attn_outproj(x, ob, wo):
    t, d = x.shape
    tm = min(t, 512)
    row = pl.BlockSpec((tm, d), lambda i: (i, 0))
    return pl.pallas_call(
        _outproj_kernel,
        grid=(t // tm,),
        in_specs=[row, row, pl.BlockSpec((d, d), lambda i: (0, 0))],
        out_specs=row,
        out_shape=jax.ShapeDtypeStruct((t, d), F32),
        compiler_params=pltpu.CompilerParams(
            dimension_semantics=("arbitrary",), vmem_limit_bytes=VMEM_LIMIT),
        name="attn_outproj",
    )(x, ob, wo)


def _trunk(x, conv_state, past_k, past_v, w):
    b, s, d = x.shape
    t = b * s
    x1, new_conv = _conv_layer(x, conv_state, w["g_mix0"], w["conv_w_in"], w["conv_w"], w["conv_w_out"])
    x2 = _peer_layer(x1.reshape(t, d), w["g_ffn0"], w["g_final"], *w["peer0"], final=False)
    k, v, kb, vb, qb = _kvq_proj(x2, w["g_kv"], w["g_mix1"], w["attn_wk"], w["attn_wv"], w["attn_wq"])
    shp = (b, s, d)
    if past_k is None:
        ob = _attn_prompt(qb.reshape(shp), kb.reshape(shp), vb.reshape(shp), w["attn_lambda"], w["attn_subln"])
    else:
        ob = _attn_sample(qb.reshape(shp), kb.reshape(shp), vb.reshape(shp),
                          past_k.reshape(b, -1, d), past_v.reshape(b, -1, d),
                          w["attn_lambda"], w["attn_subln"])
    x3 = _attn_outproj(x2, ob.reshape(t, d), w["attn_wo"])
    y = _peer_layer(x3, w["g_ffn1"], w["g_final"], *w["peer1"], final=True)
    return (y.reshape(shp), k.reshape(b, s, N_HEADS, 2, HEAD_DIM), v.reshape(b, s, N_HEADS, HEAD_W),
            new_conv[None])


def kernel(x_prompt, x_sample, cache_k, cache_v, state_conv, g_mix, g_ffn, conv_w_in, conv_w, conv_w_out, g_kv, attn_wk, attn_wv, attn_wq, attn_lambda, attn_subln, attn_wo, peer_wq, peer_keys, peer_u, peer_v, g_final):
    d = D_MODEL

    def peer_weights(l):
        return (peer_wq[l].T.astype(BF16),
                peer_keys[l].reshape(2 * PEER_HEADS, N_KEYS, N_KEYS).astype(BF16),
                peer_u[l].astype(BF16),
                peer_v[l].astype(BF16).reshape(N_EXPERTS // PEER_VBLOCK, PEER_VBLOCK, d).transpose(0, 2, 1))

    w = {
        "g_mix0": g_mix[0].reshape(1, d), "g_mix1": g_mix[1].reshape(1, d),
        "g_ffn0": g_ffn[0].reshape(1, d), "g_ffn1": g_ffn[1].reshape(1, d),
        "g_kv": g_kv.reshape(1, d), "g_final": g_final.reshape(1, d),
        "conv_w_in": conv_w_in[0].astype(BF16), "conv_w": conv_w[0], "conv_w_out": conv_w_out[0].astype(BF16),
        "attn_wk": attn_wk.astype(BF16), "attn_wv": attn_wv.astype(BF16), "attn_wq": attn_wq[0].astype(BF16),
        "attn_lambda": attn_lambda[0], "attn_subln": attn_subln[0].reshape(1, HEAD_W),
        "attn_wo": attn_wo[0].astype(BF16),
        "peer0": peer_weights(0), "peer1": peer_weights(1),
    }
    zero_state = jnp.zeros((x_prompt.shape[0], CONV_W - 1, d), F32)
    y_p, k_p, v_p, conv_p = _trunk(x_prompt, zero_state, None, None, w)
    y_s, k_s, v_s, conv_s = _trunk(x_sample, state_conv[0], cache_k, cache_v, w)
    return (y_p, y_s, k_p, v_p, conv_p, k_s, v_s, conv_s)
```

```python
import functools
import math

import jax
import jax.numpy as jnp
from jax import lax
from jax.experimental import pallas as pl
from jax.experimental.pallas import tpu as pltpu

D_MODEL = 1024
CHUNK = 64
CONV_W = 3
HEAD_DIM = 64
N_HEADS = 8
HEAD_W = 2 * HEAD_DIM
PEER_HEADS = 8
N_KEYS = 128
N_EXPERTS = N_KEYS * N_KEYS
PEER_TOPK = 16
PEER_QDIM = 256
EPS = 1e-6
LAMBDA_INIT_L1 = 0.8 - 0.6 * math.exp(-0.3 * 1)

LANES = 128
VMEM_LIMIT = 56 * 1024 * 1024

F32 = jnp.float32
BF16 = jnp.bfloat16
NEG_INF = float("-inf")


def _rms_rows(x, g):
    return x * lax.rsqrt(jnp.mean(x * x, axis=-1, keepdims=True) + EPS) * g


def _conv_kernel(x_ref, st_ref, g_ref, win_ref, cw_ref, wout_ref, o_ref, ns_ref, zp_ref, *, ts):
    s = pl.program_id(1)
    d = D_MODEL

    @pl.when(s == 0)
    def _():
        zp_ref[0:8, :] = jnp.zeros((8, d), F32)
        zp_ref[6:8, :] = st_ref[0]

    x = x_ref[0]
    h = _rms_rows(x, g_ref[...]).astype(BF16)
    p = jnp.dot(h, win_ref[...], preferred_element_type=F32)
    z = p[:, d:2 * d] * p[:, 2 * d:]
    zp_ref[8:8 + ts, :] = z
    conv = (cw_ref[0:1, :] * zp_ref[6:6 + ts, :] + cw_ref[1:2, :] * zp_ref[7:7 + ts, :]
            + cw_ref[2:3, :] * z)
    y = jnp.dot((p[:, :d] * conv).astype(BF16), wout_ref[...], preferred_element_type=F32)
    o_ref[0] = x + y
    ns_ref[0] = zp_ref[ts + 6:ts + 8, :]
    zp_ref[0:8, :] = zp_ref[ts:ts + 8, :]


def _conv_layer(x, state, g, w_in, conv_w, w_out):
    b, s, d = x.shape
    ts = min(s, 512)
    return pl.pallas_call(
        functools.partial(_conv_kernel, ts=ts),
        grid=(b, s // ts),
        in_specs=[
            pl.BlockSpec((1, ts, d), lambda i, j: (i, j, 0)),
            pl.BlockSpec((1, CONV_W - 1, d), lambda i, j: (i, 0, 0)),
            pl.BlockSpec((1, d), lambda i, j: (0, 0)),
            pl.BlockSpec((d, 3 * d), lambda i, j: (0, 0)),
            pl.BlockSpec((CONV_W, d), lambda i, j: (0, 0)),
            pl.BlockSpec((d, d), lambda i, j: (0, 0)),
        ],
        out_specs=[
            pl.BlockSpec((1, ts, d), lambda i, j: (i, j, 0)),
            pl.BlockSpec((1, CONV_W - 1, d), lambda i, j: (i, 0, 0)),
        ],
        out_shape=[
            jax.ShapeDtypeStruct((b, s, d), F32),
            jax.ShapeDtypeStruct((b, CONV_W - 1, d), F32),
        ],
        scratch_shapes=[pltpu.VMEM((ts + 8, d), F32)],
        compiler_params=pltpu.CompilerParams(
            dimension_semantics=("arbitrary", "arbitrary"), vmem_limit_bytes=VMEM_LIMIT),
        name="conv_layer",
    )(x, state, g, w_in, conv_w, w_out)


def _sort_network(n):
    pairs = []
    p = 1
    while p < n:
        k = p
        while k >= 1:
            for j in range(k % p, n - k, 2 * k):
                for i in range(min(k, n - j - k)):
                    if (i + j) // (2 * p) == (i + j + k) // (2 * p):
                        pairs.append((i + j, i + j + k))
            k //= 2
        p *= 2
    return pairs


def _top16(s):
    sub = s.shape[0] // PEER_TOPK
    v = [s[sub * k:sub * (k + 1)] for k in range(PEER_TOPK)]
    for i, j in _sort_network(PEER_TOPK):
        v[i], v[j] = jnp.maximum(v[i], v[j]), jnp.minimum(v[i], v[j])
    shift = sub // 2
    while shift >= 1:
        v = [jnp.maximum(v[k], pltpu.roll(v[PEER_TOPK - 1 - k], shift, 0)) for k in range(PEER_TOPK)]
        d = PEER_TOPK // 2
        while d >= 1:
            for k in range(PEER_TOPK):
                if k & d == 0:
                    v[k], v[k + d] = jnp.maximum(v[k], v[k + d]), jnp.minimum(v[k], v[k + d])
            d //= 2
        shift //= 2
    return [t[0:1] for t in v]


def _peer_route(s0, s1):
    v0 = _top16(s0)
    v1 = _top16(s1)
    rank1 = jnp.full(s1.shape, float(PEER_TOPK), F32)
    for b in reversed(range(PEER_TOPK)):
        rank1 = jnp.where(s1 >= v1[b], float(b), rank1)
    v0a = jnp.concatenate(v0, axis=0)
    v1a = jnp.concatenate(v1, axis=0)
    cand = jnp.concatenate(
        [v0[0] + v1a] + [v0[a] + v1a[0:8] for a in range(1, 8)] + [v0a[8:16] + v1[0]], axis=0)
    top = v0[0] + v1[0]
    c = cand
    tau = top
    for _ in range(PEER_TOPK):
        tau = jnp.max(c, axis=0, keepdims=True)
        c = jnp.where(c == tau, NEG_INF, c)
    z = jnp.sum(jnp.where(cand >= tau, jnp.exp(cand - top), 0.0), axis=0, keepdims=True)
    cnt = jnp.zeros(s0.shape, F32)
    low = 4
    for b in range(low):
        cnt = cnt + jnp.where(s0 + v1[b] >= tau, 1.0, 0.0)
    for a in range(PEER_TOPK // (low + 1)):
        hi = jnp.zeros(tau.shape, F32)
        for b in range(low, PEER_TOPK // (a + 1)):
            hi = hi + jnp.where(v0[a] + v1[b] >= tau, 1.0, 0.0)
        cnt = cnt + jnp.where(s0 == v0[a], hi, 0.0)
    a_w = jnp.exp(s0 - v0[0]) / z
    b_w = jnp.exp(s1 - v1[0])
    return cnt, rank1, a_w, b_w


def _peer_kernel(x_ref, g_ref, gf_ref, wqt_ref, keys_ref, u_ref, vt_ref, o_ref,
                 ht_ref, sc_ref, rk_ref, bw_ref, cn_ref, aw_ref, acc_ref, a_ref, w_ref,
                 *, tm, ne, final):
    e = pl.program_id(1)
    n_e = pl.num_programs(1)
    rows_per_step = ne // N_KEYS
    n_chunks = tm // LANES

    @pl.when(e == 0)
    def _prologue():
        h = _rms_rows(x_ref[...], g_ref[...])
        ht_ref[...] = h.T.astype(BF16)
        qt = jnp.dot(wqt_ref[...], ht_ref[...], preferred_element_type=F32).astype(BF16)
        for hc in range(2 * PEER_HEADS):
            sc_ref[hc] = jnp.dot(keys_ref[hc], qt[hc * N_KEYS:(hc + 1) * N_KEYS],
                                 preferred_element_type=F32)

        def route_body(idx, carry):
            hd = idx // n_chunks
            ln = pl.ds(pl.multiple_of((idx % n_chunks) * LANES, LANES), LANES)
            cnt, rank1, a_w, b_w = _peer_route(sc_ref[2 * hd, :, ln], sc_ref[2 * hd + 1, :, ln])
            cn_ref[hd, :, ln] = cnt
            aw_ref[hd, :, ln] = a_w
            rk_ref[hd, :, ln] = rank1.astype(BF16)
            bw_ref[hd, :, ln] = b_w.astype(BF16)
            return carry

        lax.fori_loop(0, PEER_HEADS * n_chunks, route_body, 0)
        acc_ref[...] = jnp.zeros(acc_ref.shape, F32)

    a_ref[...] = jnp.dot(u_ref[...], ht_ref[...], preferred_element_type=F32)
    pack = (N_KEYS // 16, 16, LANES)
    for r in range(rows_per_step):
        i0 = pl.ds(pl.multiple_of(e * rows_per_step + (r // 8) * 8, 8), 8)
        rows = slice(r * N_KEYS, (r + 1) * N_KEYS)
        for tc in range(n_chunks):
            ln = slice(tc * LANES, (tc + 1) * LANES)
            gate = jnp.zeros(pack, BF16)
            for hd in range(PEER_HEADS):
                c_b = jnp.broadcast_to(cn_ref[hd, i0, ln][r % 8:r % 8 + 1], (16, LANES)).astype(BF16)
                a_b = jnp.broadcast_to(aw_ref[hd, i0, ln][r % 8:r % 8 + 1], (16, LANES)).astype(BF16)
                sel = jnp.where(rk_ref[hd, :, ln].reshape(pack) < c_b[None],
                                bw_ref[hd, :, ln].reshape(pack), jnp.zeros((), BF16))
                gate = gate + sel * a_b[None]
            pre = a_ref[rows, ln]
            act = 0.5 * pre * (1.0 + lax.erf(pre * (2.0 ** -0.5)))
            w_ref[rows, ln] = gate.reshape(N_KEYS, LANES) * act.astype(BF16)
    acc_ref[...] += jnp.dot(vt_ref[...], w_ref[...], preferred_element_type=F32)

    @pl.when(e == n_e - 1)
    def _epilogue():
        y = x_ref[...] + acc_ref[...].T
        if final:
            y = _rms_rows(y, gf_ref[...])
        o_ref[...] = y


def _peer_layer(x, g, g_final, wq_t, keys, u, v_t, *, final):
    t, d = x.shape
    tm = min(t, 512)
    ne = 1024
    return pl.pallas_call(
        functools.partial(_peer_kernel, tm=tm, ne=ne, final=final),
        grid=(t // tm, N_EXPERTS // ne),
        in_specs=[
            pl.BlockSpec((tm, d), lambda i, e: (i, 0)),
            pl.BlockSpec((1, d), lambda i, e: (0, 0)),
            pl.BlockSpec((1, d), lambda i, e: (0, 0)),
            pl.BlockSpec((PEER_HEADS * PEER_QDIM, d), lambda i, e: (0, 0)),
            pl.BlockSpec((2 * PEER_HEADS, N_KEYS, N_KEYS), lambda i, e: (0, 0, 0)),
            pl.BlockSpec((ne, d), lambda i, e: (e, 0)),
            pl.BlockSpec((d, ne), lambda i, e: (0, e)),
        ],
        out_specs=pl.BlockSpec((tm, d), lambda i, e: (i, 0)),
        out_shape=jax.ShapeDtypeStruct((t, d), F32),
        scratch_shapes=[
            pltpu.VMEM((d, tm), BF16),
            pltpu.VMEM((2 * PEER_HEADS, N_KEYS, tm), F32),
            pltpu.VMEM((PEER_HEADS, N_KEYS, tm), BF16),
            pltpu.VMEM((PEER_HEADS, N_KEYS, tm), BF16),
            pltpu.VMEM((PEER_HEADS, N_KEYS, tm), F32),
            pltpu.VMEM((PEER_HEADS, N_KEYS, tm), F32),
            pltpu.VMEM((d, tm), F32),
            pltpu.VMEM((ne, tm), F32),
            pltpu.VMEM((ne, tm), BF16),
        ],
        compiler_params=pltpu.CompilerParams(
            dimension_semantics=("arbitrary", "arbitrary"), vmem_limit_bytes=VMEM_LIMIT),
        name="peer_final" if final else "peer",
    )(x, g, g_final, wq_t, keys, u, v_t)


def _kvq_kernel(x_ref, gkv_ref, gq_ref, wk_ref, wv_ref, wq_ref,
                k_ref, v_ref, kb_ref, vb_ref, qb_ref):
    x = x_ref[...]
    n = x * lax.rsqrt(jnp.mean(x * x, axis=-1, keepdims=True) + EPS)
    hk = (n * gkv_ref[...]).astype(BF16)
    hq = (n * gq_ref[...]).astype(BF16)
    k = jnp.dot(hk, wk_ref[...], preferred_element_type=F32)
    v = jnp.dot(hk, wv_ref[...], preferred_element_type=F32)
    q = jnp.dot(hq, wq_ref[...], preferred_element_type=F32)
    k_ref[...] = k
    v_ref[...] = v
    kb_ref[...] = k.astype(BF16)
    vb_ref[...] = v.astype(BF16)
    qb_ref[...] = (q * (HEAD_DIM ** -0.5)).astype(BF16)


def _kvq_proj(x, g_kv, g_q, wk, wv, wq):
    t, d = x.shape
    tm = min(t, 512)
    row = pl.BlockSpec((tm, d), lambda i: (i, 0))
    vec = pl.BlockSpec((1, d), lambda i: (0, 0))
    mat = pl.BlockSpec((d, d), lambda i: (0, 0))
    return pl.pallas_call(
        _kvq_kernel,
        grid=(t // tm,),
        in_specs=[row, vec, vec, mat, mat, mat],
        out_specs=[row] * 5,
        out_shape=[jax.ShapeDtypeStruct((t, d), F32)] * 2 + [jax.ShapeDtypeStruct((t, d), BF16)] * 3,
        compiler_params=pltpu.CompilerParams(
            dimension_semantics=("arbitrary",), vmem_limit_bytes=VMEM_LIMIT),
        name="kvq_proj",
    )(x, g_kv, g_q, wk, wv, wq)


def _lambda_full(lam_ref):
    lp = lam_ref[...]
    s01 = jnp.sum(lp[0:1] * lp[1:2], axis=-1, keepdims=True)
    s23 = jnp.sum(lp[2:3] * lp[3:4], axis=-1, keepdims=True)
    return jnp.exp(s01) - jnp.exp(s23) + LAMBDA_INIT_L1


def _stack_maps(q):
    lane = lax.broadcasted_iota(jnp.int32, q.shape, 1)
    zero = jnp.zeros((), q.dtype)
    return jnp.concatenate([jnp.where(lane < HEAD_DIM, q, zero), jnp.where(lane >= HEAD_DIM, q, zero)],
                           axis=0)


def _flash_step(qs, k, v, m, l, acc, mask):
    s = lax.dot_general(qs, k, (((1,), (1,)), ((), ())), preferred_element_type=F32)
    if mask is not None:
        s = jnp.where(mask, s, NEG_INF)
    m_new = jnp.maximum(m, jnp.max(s, axis=-1, keepdims=True))
    alpha = jnp.exp(m - m_new)
    p = jnp.exp(s - m_new)
    l = alpha * l + jnp.sum(p, axis=-1, keepdims=True)
    acc = alpha * acc + jnp.dot(p.astype(BF16), v, preferred_element_type=F32)
    return m_new, l, acc


def _diff_finish(l, acc, lam, subln, n):
    o = acc / l
    o = o[:n] - lam * o[n:]
    o = o * lax.rsqrt(jnp.mean(o * o, axis=-1, keepdims=True) + EPS) * subln
    return (o * (1.0 - LAMBDA_INIT_L1)).astype(BF16)


def _attn_prompt_kernel(q_ref, k_ref, v_ref, lam_ref, sub_ref, o_ref,
                        vt_ref, qt_ref, sa_ref, sb_ref, acc_ref, *, tq, tk):
    qi = pl.program_id(2)
    n2 = 2 * tq
    seq = v_ref.shape[1]

    @pl.when(qi == 0)
    def _():
        for c in range(seq // 512):
            rows = slice(c * 512, (c + 1) * 512)
            vt_ref[:, rows] = v_ref[0, rows, :].astype(F32).T.astype(BF16)

    qt_ref[...] = _stack_maps(q_ref[0]).astype(F32).T.astype(BF16)
    acc_ref[...] = jnp.zeros(acc_ref.shape, F32)

    def scores(t):
        rows = pl.ds(pl.multiple_of(t * tk, tk), tk)
        return jnp.dot(k_ref[0, rows, :], qt_ref[...], preferred_element_type=F32)

    def consume(s, t, m, l):
        cols = pl.ds(pl.multiple_of(t * tk, tk), tk)
        m_new = jnp.maximum(m, jnp.max(s, axis=0, keepdims=True))
        alpha = jnp.exp(m - m_new)
        p = jnp.exp(s - m_new)
        l = alpha * l + jnp.sum(p, axis=0, keepdims=True)
        acc_ref[...] = alpha * acc_ref[...] + jnp.dot(vt_ref[:, cols], p.astype(BF16),
                                                      preferred_element_type=F32)
        return m_new, l

    def pair(p, carry):
        m, l = carry
        sb_ref[...] = scores(2 * p + 1)
        m, l = consume(sa_ref[...], 2 * p, m, l)
        sa_ref[...] = scores(2 * p + 2)
        return consume(sb_ref[...], 2 * p + 1, m, l)

    def diag_bias(s, half):
        qc = (lax.broadcasted_iota(jnp.int32, (1, n2), 1) % tq) // CHUNK
        parts = []
        for c in range(tk // CHUNK):
            kc = half * (tk // CHUNK) + c
            bias = jnp.where(qc >= kc, 0.0, NEG_INF)
            parts.append(s[c * CHUNK:(c + 1) * CHUNK] + bias)
        return jnp.concatenate(parts, axis=0)

    sa_ref[...] = scores(0)
    m0 = jnp.full((1, n2), NEG_INF, F32)
    l0 = jnp.zeros((1, n2), F32)
    m, l = lax.fori_loop(0, qi, pair, (m0, l0))
    t0 = qi * (tq // tk)
    sb_ref[...] = scores(t0 + 1)
    m, l = consume(diag_bias(sa_ref[...], 0), t0, m, l)
    m, l = consume(diag_bias(sb_ref[...], 1), t0 + 1, m, l)

    o = acc_ref[...] * (1.0 / l)
    o = o[:, :tq] - _lambda_full(lam_ref) * o[:, tq:]
    o = o * lax.rsqrt(jnp.mean(o * o, axis=0, keepdims=True) + EPS)
    o_ref[0] = (o.T * sub_ref[...] * (1.0 - LAMBDA_INIT_L1)).astype(BF16)


def _attn_prompt(qb, kb, vb, lam_p, subln):
    b, s, d = qb.shape
    tq = 512
    tk = tq // 2
    return pl.pallas_call(
        functools.partial(_attn_prompt_kernel, tq=tq, tk=tk),
        grid=(b, N_HEADS, s // tq),
        in_specs=[
            pl.BlockSpec((1, tq, HEAD_W), lambda i, h, j: (i, j, h)),
            pl.BlockSpec((1, s, HEAD_W), lambda i, h, j: (i, 0, h)),
            pl.BlockSpec((1, s, HEAD_W), lambda i, h, j: (i, 0, h)),
            pl.BlockSpec((4, HEAD_DIM), lambda i, h, j: (0, 0)),
            pl.BlockSpec((1, HEAD_W), lambda i, h, j: (0, 0)),
        ],
        out_specs=pl.BlockSpec((1, tq, HEAD_W), lambda i, h, j: (i, j, h)),
        out_shape=jax.ShapeDtypeStruct((b, s, d), BF16),
        scratch_shapes=[
            pltpu.VMEM((HEAD_W, s), BF16),
            pltpu.VMEM((HEAD_W, 2 * tq), BF16),
            pltpu.VMEM((tk, 2 * tq), F32),
            pltpu.VMEM((tk, 2 * tq), F32),
            pltpu.VMEM((HEAD_W, 2 * tq), F32),
        ],
        compiler_params=pltpu.CompilerParams(
            dimension_semantics=("arbitrary", "arbitrary", "arbitrary"), vmem_limit_bytes=VMEM_LIMIT),
        name="attn_prompt",
    )(qb, kb, vb, lam_p, subln)


def _attn_sample_kernel(q_ref, kn_ref, vn_ref, ck_ref, cv_ref, lam_ref, sub_ref, o_ref,
                        m_ref, l_ref, acc_ref, *, n):
    j = pl.program_id(1)
    last = pl.num_programs(1) - 1

    @pl.when(j == 0)
    def _():
        m_ref[...] = jnp.full(m_ref.shape, NEG_INF, F32)
        l_ref[...] = jnp.zeros(l_ref.shape, F32)
        acc_ref[...] = jnp.zeros(acc_ref.shape, F32)

    for h in range(N_HEADS):
        ln = slice(h * HEAD_W, (h + 1) * HEAD_W)
        qs = _stack_maps(q_ref[0, :, ln])
        m, l, acc = _flash_step(qs, ck_ref[0, :, ln].astype(BF16), cv_ref[0, :, ln].astype(BF16),
                                m_ref[h], l_ref[h], acc_ref[h], None)
        m_ref[h] = m
        l_ref[h] = l
        acc_ref[h] = acc

    @pl.when(j == last)
    def _():
        lam = _lambda_full(lam_ref)
        for h in range(N_HEADS):
            ln = slice(h * HEAD_W, (h + 1) * HEAD_W)
            qs = _stack_maps(q_ref[0, :, ln])
            m, l, acc = _flash_step(qs, kn_ref[0, :, ln], vn_ref[0, :, ln],
                                    m_ref[h], l_ref[h], acc_ref[h], None)
            o_ref[0, :, ln] = _diff_finish(l, acc, lam, sub_ref[...], n)


def _attn_sample(qb, kb, vb, cache_k, cache_v, lam_p, subln):
    b, n, d = qb.shape
    past = cache_k.shape[1]
    tk = min(past, 1024)
    new = pl.BlockSpec((1, n, d), lambda i, j: (i, 0, 0))
    old = pl.BlockSpec((1, tk, d), lambda i, j: (i, j, 0))
    return pl.pallas_call(
        functools.partial(_attn_sample_kernel, n=n),
        grid=(b, past // tk),
        in_specs=[new, new, new, old, old,
                  pl.BlockSpec((4, HEAD_DIM), lambda i, j: (0, 0)),
                  pl.BlockSpec((1, HEAD_W), lambda i, j: (0, 0))],
        out_specs=new,
        out_shape=jax.ShapeDtypeStruct((b, n, d), BF16),
        scratch_shapes=[
            pltpu.VMEM((N_HEADS, 2 * n, 1), F32),
            pltpu.VMEM((N_HEADS, 2 * n, 1), F32),
            pltpu.VMEM((N_HEADS, 2 * n, HEAD_W), F32),
        ],
        compiler_params=pltpu.CompilerParams(
            dimension_semantics=("arbitrary", "arbitrary"), vmem_limit_bytes=VMEM_LIMIT),
        name="attn_sample",
    )(qb, kb, vb, cache_k, cache_v, lam_p, subln)


def _outproj_kernel(x_ref, o_ref, wo_ref, y_ref):
    y_ref[...] = x_ref[...] + jnp.dot(o_ref[...], wo_ref[...], preferred_element_type=F32)


def _attn_outproj(x, ob, wo):
    t, d = x.shape
    tm = min(t, 512)
    row = pl.BlockSpec((tm, d), lambda i: (i, 0))
    return pl.pallas_call(
        _outproj_kernel,
        grid=(t // tm,),
        in_specs=[row, row, pl.BlockSpec((d, d), lambda i: (0, 0))],
        out_specs=row,
        out_shape=jax.ShapeDtypeStruct((t, d), F32),
        compiler_params=pltpu.CompilerParams(
            dimension_semantics=("arbitrary",), vmem_limit_bytes=VMEM_LIMIT),
        name="attn_outproj",
    )(x, ob, wo)


def _trunk(x, conv_state, past_k, past_v, w):
    b, s, d = x.shape
    t = b * s
    x1, new_conv = _conv_layer(x, conv_state, w["g_mix0"], w["conv_w_in"], w["conv_w"], w["conv_w_out"])
    x2 = _peer_layer(x1.reshape(t, d), w["g_ffn0"], w["g_final"], *w["peer0"], final=False)
    k, v, kb, vb, qb = _kvq_proj(x2, w["g_kv"], w["g_mix1"], w["attn_wk"], w["attn_wv"], w["attn_wq"])
    shp = (b, s, d)
    if past_k is None:
        ob = _attn_prompt(qb.reshape(shp), kb.reshape(shp), vb.reshape(shp), w["attn_lambda"], w["attn_subln"])
    else:
        ob = _attn_sample(qb.reshape(shp), kb.reshape(shp), vb.reshape(shp),
                          past_k.reshape(b, -1, d), past_v.reshape(b, -1, d),
                          w["attn_lambda"], w["attn_subln"])
    x3 = _attn_outproj(x2, ob.reshape(t, d), w["attn_wo"])
    y = _peer_layer(x3, w["g_ffn1"], w["g_final"], *w["peer1"], final=True)
    return (y.reshape(shp), k.reshape(b, s, N_HEADS, 2, HEAD_DIM), v.reshape(b, s, N_HEADS, HEAD_W),
            new_conv[None])


def kernel(x_prompt, x_sample, cache_k, cache_v, state_conv, g_mix, g_ffn, conv_w_in, conv_w, conv_w_out, g_kv, attn_wk, attn_wv, attn_wq, attn_lambda, attn_subln, attn_wo, peer_wq, peer_keys, peer_u, peer_v, g_final):
    d = D_MODEL

    def peer_weights(l):
        return (peer_wq[l].T.astype(BF16),
                peer_keys[l].reshape(2 * PEER_HEADS, N_KEYS, N_KEYS).astype(BF16),
                peer_u[l].astype(BF16),
                peer_v[l].T.astype(BF16))

    w = {
        "g_mix0": g_mix[0].reshape(1, d), "g_mix1": g_mix[1].reshape(1, d),
        "g_ffn0": g_ffn[0].reshape(1, d), "g_ffn1": g_ffn[1].reshape(1, d),
        "g_kv": g_kv.reshape(1, d), "g_final": g_final.reshape(1, d),
        "conv_w_in": conv_w_in[0].astype(BF16), "conv_w": conv_w[0], "conv_w_out": conv_w_out[0].astype(BF16),
        "attn_wk": attn_wk.astype(BF16), "attn_wv": attn_wv.astype(BF16), "attn_wq": attn_wq[0].astype(BF16),
        "attn_lambda": attn_lambda[0], "attn_subln": attn_subln[0].reshape(1, HEAD_W),
        "attn_wo": attn_wo[0].astype(BF16),
        "peer0": peer_weights(0), "peer1": peer_weights(1),
    }
    zero_state = jnp.zeros((x_prompt.shape[0], CONV_W - 1, d), F32)
    y_p, k_p, v_p, conv_p = _trunk(x_prompt, zero_state, None, None, w)
    y_s, k_s, v_s, conv_s = _trunk(x_sample, state_conv[0], cache_k, cache_v, w)
    return (y_p, y_s, k_p, v_p, conv_p, k_s, v_s, conv_s)
```

```python
import functools
import math

import jax
import jax.numpy as jnp
from jax import lax
from jax.experimental import pallas as pl
from jax.experimental.pallas import tpu as pltpu

D_MODEL = 1024
CHUNK = 64
CONV_W = 3
HEAD_DIM = 64
N_HEADS = 8
HEAD_W = 2 * HEAD_DIM
PEER_HEADS = 8
N_KEYS = 128
N_EXPERTS = N_KEYS * N_KEYS
PEER_TOPK = 16
PEER_QDIM = 256
BW_SKEW = 16
EPS = 1e-6
LAMBDA_INIT_L1 = 0.8 - 0.6 * math.exp(-0.3 * 1)
QUERY_SCALE = HEAD_DIM ** -0.5 * math.log2(math.e)

LANES = 128
VMEM_LIMIT = 56 * 1024 * 1024

F32 = jnp.float32
BF16 = jnp.bfloat16
NEG_INF = float("-inf")


def _rms_rows(x, g):
    return x * lax.rsqrt(jnp.mean(x * x, axis=-1, keepdims=True) + EPS) * g


def _conv_kernel(x_ref, st_ref, g_ref, win_ref, cw_ref, wout_ref, o_ref, ns_ref, zp_ref, *, ts):
    s = pl.program_id(1)
    d = D_MODEL

    @pl.when(s == 0)
    def _():
        zp_ref[0:8, :] = jnp.zeros((8, d), F32)
        zp_ref[6:8, :] = st_ref[0]

    x = x_ref[0]
    h = _rms_rows(x, g_ref[...]).astype(BF16)
    p = jnp.dot(h, win_ref[...], preferred_element_type=F32)
    z = p[:, d:2 * d] * p[:, 2 * d:]
    zp_ref[8:8 + ts, :] = z
    conv = (cw_ref[0:1, :] * zp_ref[6:6 + ts, :] + cw_ref[1:2, :] * zp_ref[7:7 + ts, :]
            + cw_ref[2:3, :] * z)
    y = jnp.dot((p[:, :d] * conv).astype(BF16), wout_ref[...], preferred_element_type=F32)
    o_ref[0] = x + y
    ns_ref[0] = zp_ref[ts + 6:ts + 8, :]
    zp_ref[0:8, :] = zp_ref[ts:ts + 8, :]


def _conv_layer(x, state, g, w_in, conv_w, w_out):
    b, s, d = x.shape
    ts = min(s, 512)
    return pl.pallas_call(
        functools.partial(_conv_kernel, ts=ts),
        grid=(b, s // ts),
        in_specs=[
            pl.BlockSpec((1, ts, d), lambda i, j: (i, j, 0)),
            pl.BlockSpec((1, CONV_W - 1, d), lambda i, j: (i, 0, 0)),
            pl.BlockSpec((1, d), lambda i, j: (0, 0)),
            pl.BlockSpec((d, 3 * d), lambda i, j: (0, 0)),
            pl.BlockSpec((CONV_W, d), lambda i, j: (0, 0)),
            pl.BlockSpec((d, d), lambda i, j: (0, 0)),
        ],
        out_specs=[
            pl.BlockSpec((1, ts, d), lambda i, j: (i, j, 0)),
            pl.BlockSpec((1, CONV_W - 1, d), lambda i, j: (i, 0, 0)),
        ],
        out_shape=[
            jax.ShapeDtypeStruct((b, s, d), F32),
            jax.ShapeDtypeStruct((b, CONV_W - 1, d), F32),
        ],
        scratch_shapes=[pltpu.VMEM((ts + 8, d), F32)],
        compiler_params=pltpu.CompilerParams(
            dimension_semantics=("arbitrary", "arbitrary"), vmem_limit_bytes=VMEM_LIMIT),
        name="conv_layer",
    )(x, state, g, w_in, conv_w, w_out)


def _sort_network(n):
    pairs = []
    p = 1
    while p < n:
        k = p
        while k >= 1:
            for j in range(k % p, n - k, 2 * k):
                for i in range(min(k, n - j - k)):
                    if (i + j) // (2 * p) == (i + j + k) // (2 * p):
                        pairs.append((i + j, i + j + k))
            k //= 2
        p *= 2
    return pairs


def _top16(s):
    sub = s.shape[0] // PEER_TOPK
    v = [s[sub * k:sub * (k + 1)] for k in range(PEER_TOPK)]
    for i, j in _sort_network(PEER_TOPK):
        v[i], v[j] = jnp.maximum(v[i], v[j]), jnp.minimum(v[i], v[j])
    shift = sub // 2
    while shift >= 1:
        v = [jnp.maximum(v[k], pltpu.roll(v[PEER_TOPK - 1 - k], shift, 0)) for k in range(PEER_TOPK)]
        d = PEER_TOPK // 2
        while d >= 1:
            for k in range(PEER_TOPK):
                if k & d == 0:
                    v[k], v[k + d] = jnp.maximum(v[k], v[k + d]), jnp.minimum(v[k], v[k + d])
            d //= 2
        shift //= 2
    return [t[0:1] for t in v]


def _peer_route(s0, s1):
    v0 = _top16(s0)
    v1 = _top16(s1)
    rank1 = jnp.full(s1.shape, float(PEER_TOPK), F32)
    for b in reversed(range(PEER_TOPK)):
        rank1 = jnp.where(s1 >= v1[b], float(b), rank1)
    v0a = jnp.concatenate(v0, axis=0)
    v1a = jnp.concatenate(v1, axis=0)
    cand = jnp.concatenate(
        [v0[0] + v1a] + [v0[a] + v1a[0:8] for a in range(1, 8)] + [v0a[8:16] + v1[0]], axis=0)
    top = v0[0] + v1[0]
    pad = jnp.full((s0.shape[0] - cand.shape[0], cand.shape[1]), NEG_INF, F32)
    tau = _top16(jnp.concatenate([cand, pad], axis=0))[PEER_TOPK - 1]
    z = jnp.sum(jnp.where(cand >= tau, jnp.exp(cand - top), 0.0), axis=0, keepdims=True)
    cnt = jnp.zeros(s0.shape, F32)
    low = 4
    for b in range(low):
        cnt = cnt + jnp.where(s0 + v1[b] >= tau, 1.0, 0.0)
    for a in range(PEER_TOPK // (low + 1)):
        hi = jnp.zeros(tau.shape, F32)
        for b in range(low, PEER_TOPK // (a + 1)):
            hi = hi + jnp.where(v0[a] + v1[b] >= tau, 1.0, 0.0)
        cnt = cnt + jnp.where(s0 == v0[a], hi, 0.0)
    a_w = jnp.exp(s0 - v0[0]) * (0.5 / z)
    b_w = jnp.exp(s1 - v1[0])
    return cnt, rank1, a_w, b_w


def _peer_kernel(x_ref, g_ref, gf_ref, wqt_ref, keys_ref, u_ref, vt_ref, o_ref,
                 ht_ref, sc_ref, rk_ref, bw_ref, cn_ref, aw_ref, acc_ref, a_ref, w_ref,
                 *, tm, ne, final):
    e = pl.program_id(1)
    n_e = pl.num_programs(1)
    rows_per_step = ne // N_KEYS
    n_chunks = tm // LANES

    @pl.when(e == 0)
    def _prologue():
        h = _rms_rows(x_ref[...], g_ref[...])
        ht_ref[...] = h.T.astype(BF16)
        qt = jnp.dot(wqt_ref[...], ht_ref[...], preferred_element_type=F32).astype(BF16)
        for hc in range(2 * PEER_HEADS):
            sc_ref[hc] = jnp.dot(keys_ref[hc], qt[hc * N_KEYS:(hc + 1) * N_KEYS],
                                 preferred_element_type=F32)

        def route_body(idx, carry):
            hd = idx // n_chunks
            ln = pl.ds(pl.multiple_of((idx % n_chunks) * LANES, LANES), LANES)
            cnt, rank1, a_w, b_w = _peer_route(sc_ref[2 * hd, :, ln], sc_ref[2 * hd + 1, :, ln])
            cn_ref[hd, :, ln] = cnt
            aw_ref[hd, :, ln] = a_w
            rk_ref[hd, idx % n_chunks] = rank1.astype(BF16)
            bw_ref[hd, idx % n_chunks, BW_SKEW:, :] = b_w.astype(BF16)
            return carry

        lax.fori_loop(0, PEER_HEADS * n_chunks, route_body, 0)
        acc_ref[...] = jnp.zeros(acc_ref.shape, F32)

    a_ref[...] = jnp.dot(u_ref[...], ht_ref[...], preferred_element_type=F32)
    pack = (N_KEYS // 16, 16, LANES)
    for r in range(rows_per_step):
        i0 = pl.ds(pl.multiple_of(e * rows_per_step + (r // 8) * 8, 8), 8)
        rows = slice(r * N_KEYS, (r + 1) * N_KEYS)
        for tc in range(n_chunks):
            ln = slice(tc * LANES, (tc + 1) * LANES)
            gate = jnp.zeros(pack, BF16)
            for hd in range(PEER_HEADS):
                c_b = jnp.broadcast_to(cn_ref[hd, i0, ln][r % 8:r % 8 + 1], (16, LANES)).astype(BF16)
                a_b = jnp.broadcast_to(aw_ref[hd, i0, ln][r % 8:r % 8 + 1], (16, LANES)).astype(BF16)
                sel = jnp.where(rk_ref[hd, tc].reshape(pack) < c_b[None],
                                bw_ref[hd, tc, BW_SKEW:, :].reshape(pack), jnp.zeros((), BF16))
                gate = gate + sel * a_b[None]
            pre = a_ref[rows, ln]
            act = pre * (1.0 + lax.erf(pre * (2.0 ** -0.5)))
            w_ref[rows, ln] = gate.reshape(N_KEYS, LANES) * act.astype(BF16)
    acc_ref[...] += jnp.dot(vt_ref[...], w_ref[...], preferred_element_type=F32)

    @pl.when(e == n_e - 1)
    def _epilogue():
        y = x_ref[...] + acc_ref[...].T
        if final:
            y = _rms_rows(y, gf_ref[...])
        o_ref[...] = y


def _peer_layer(x, g, g_final, wq_t, keys, u, v_t, *, final):
    t, d = x.shape
    tm = min(t, 512)
    ne = 2048
    return pl.pallas_call(
        functools.partial(_peer_kernel, tm=tm, ne=ne, final=final),
        grid=(t // tm, N_EXPERTS // ne),
        in_specs=[
            pl.BlockSpec((tm, d), lambda i, e: (i, 0)),
            pl.BlockSpec((1, d), lambda i, e: (0, 0)),
            pl.BlockSpec((1, d), lambda i, e: (0, 0)),
            pl.BlockSpec((PEER_HEADS * PEER_QDIM, d), lambda i, e: (0, 0)),
            pl.BlockSpec((2 * PEER_HEADS, N_KEYS, N_KEYS), lambda i, e: (0, 0, 0)),
            pl.BlockSpec((ne, d), lambda i, e: (e, 0)),
            pl.BlockSpec((d, ne), lambda i, e: (0, e)),
        ],
        out_specs=pl.BlockSpec((tm, d), lambda i, e: (i, 0)),
        out_shape=jax.ShapeDtypeStruct((t, d), F32),
        scratch_shapes=[
            pltpu.VMEM((d, tm), BF16),
            pltpu.VMEM((2 * PEER_HEADS, N_KEYS, tm), F32),
            pltpu.VMEM((PEER_HEADS, tm // LANES, N_KEYS, LANES), BF16),
            pltpu.VMEM((PEER_HEADS, tm // LANES, BW_SKEW + N_KEYS, LANES), BF16),
            pltpu.VMEM((PEER_HEADS, N_KEYS, tm), F32),
            pltpu.VMEM((PEER_HEADS, N_KEYS, tm), F32),
            pltpu.VMEM((d, tm), F32),
            pltpu.VMEM((ne, tm), F32),
            pltpu.VMEM((ne, tm), BF16),
        ],
        compiler_params=pltpu.CompilerParams(
            dimension_semantics=("arbitrary", "arbitrary"), vmem_limit_bytes=VMEM_LIMIT),
        name="peer_final" if final else "peer",
    )(x, g, g_final, wq_t, keys, u, v_t)


def _kvq_kernel(x_ref, gkv_ref, gq_ref, wk_ref, wv_ref, wq_ref,
                k_ref, v_ref, kb_ref, vb_ref, qb_ref):
    x = x_ref[...]
    n = x * lax.rsqrt(jnp.mean(x * x, axis=-1, keepdims=True) + EPS)
    hk = (n * gkv_ref[...]).astype(BF16)
    hq = (n * gq_ref[...]).astype(BF16)
    k = jnp.dot(hk, wk_ref[...], preferred_element_type=F32)
    v = jnp.dot(hk, wv_ref[...], preferred_element_type=F32)
    q = jnp.dot(hq, wq_ref[...], preferred_element_type=F32)
    k_ref[...] = k
    v_ref[...] = v
    kb_ref[...] = k.astype(BF16)
    vb_ref[...] = v.astype(BF16)
    qb_ref[...] = (q * QUERY_SCALE).astype(BF16)


def _kvq_proj(x, g_kv, g_q, wk, wv, wq):
    t, d = x.shape
    tm = min(t, 512)
    row = pl.BlockSpec((tm, d), lambda i: (i, 0))
    vec = pl.BlockSpec((1, d), lambda i: (0, 0))
    mat = pl.BlockSpec((d, d), lambda i: (0, 0))
    return pl.pallas_call(
        _kvq_kernel,
        grid=(t // tm,),
        in_specs=[row, vec, vec, mat, mat, mat],
        out_specs=[row] * 5,
        out_shape=[jax.ShapeDtypeStruct((t, d), F32)] * 2 + [jax.ShapeDtypeStruct((t, d), BF16)] * 3,
        compiler_params=pltpu.CompilerParams(
            dimension_semantics=("arbitrary",), vmem_limit_bytes=VMEM_LIMIT),
        name="kvq_proj",
    )(x, g_kv, g_q, wk, wv, wq)


def _lambda_full(lam_ref):
    lp = lam_ref[...]
    s01 = jnp.sum(lp[0:1] * lp[1:2], axis=-1, keepdims=True)
    s23 = jnp.sum(lp[2:3] * lp[3:4], axis=-1, keepdims=True)
    return jnp.exp(s01) - jnp.exp(s23) + LAMBDA_INIT_L1


def _stack_maps(q):
    lane = lax.broadcasted_iota(jnp.int32, q.shape, 1)
    zero = jnp.zeros((), q.dtype)
    return jnp.concatenate([jnp.where(lane < HEAD_DIM, q, zero), jnp.where(lane >= HEAD_DIM, q, zero)],
                           axis=0)


def _flash_step(qs, k, v, m, l, acc, mask):
    s = lax.dot_general(qs, k, (((1,), (1,)), ((), ())), preferred_element_type=F32)
    if mask is not None:
        s = jnp.where(mask, s, NEG_INF)
    m_new = jnp.maximum(m, jnp.max(s, axis=-1, keepdims=True))
    alpha = jnp.exp2(m - m_new)
    p = jnp.exp2(s - m_new)
    l = alpha * l + jnp.sum(p, axis=-1, keepdims=True)
    acc = alpha * acc + jnp.dot(p.astype(BF16), v, preferred_element_type=F32)
    return m_new, l, acc


def _diff_finish(l, acc, lam, subln, n):
    o = acc / l
    o = o[:n] - lam * o[n:]
    o = o * lax.rsqrt(jnp.mean(o * o, axis=-1, keepdims=True) + EPS) * subln
    return (o * (1.0 - LAMBDA_INIT_L1)).astype(BF16)


def _attn_prompt_kernel(q_ref, k_ref, v_ref, lam_ref, sub_ref, o_ref,
                        vt_ref, qt_ref, sa_ref, sb_ref, acc_ref, *, tq, tk):
    qi = pl.program_id(2)
    n2 = 2 * tq
    seq = v_ref.shape[1]

    @pl.when(qi == 0)
    def _():
        for c in range(seq // 512):
            rows = slice(c * 512, (c + 1) * 512)
            vt_ref[:, rows] = v_ref[0, rows, :].astype(F32).T.astype(BF16)

    qt_ref[...] = _stack_maps(q_ref[0]).astype(F32).T.astype(BF16)
    acc_ref[...] = jnp.zeros(acc_ref.shape, F32)

    def scores(t):
        rows = pl.ds(pl.multiple_of(t * tk, tk), tk)
        return jnp.dot(k_ref[0, rows, :], qt_ref[...], preferred_element_type=F32)

    def consume(s, t, m, l):
        cols = pl.ds(pl.multiple_of(t * tk, tk), tk)
        m_new = jnp.maximum(m, jnp.max(s, axis=0, keepdims=True))
        alpha = jnp.exp2(m - m_new)
        p = jnp.exp2(s - m_new)
        l = alpha * l + jnp.sum(p, axis=0, keepdims=True)
        acc_ref[...] = alpha * acc_ref[...] + jnp.dot(vt_ref[:, cols], p.astype(BF16),
                                                      preferred_element_type=F32)
        return m_new, l

    def pair(p, carry):
        m, l = carry
        sb_ref[...] = scores(2 * p + 1)
        m, l = consume(sa_ref[...], 2 * p, m, l)
        sa_ref[...] = scores(2 * p + 2)
        return consume(sb_ref[...], 2 * p + 1, m, l)

    def diag_bias(s, half):
        qc = (lax.broadcasted_iota(jnp.int32, (1, n2), 1) % tq) // CHUNK
        parts = []
        for c in range(tk // CHUNK):
            kc = half * (tk // CHUNK) + c
            bias = jnp.where(qc >= kc, 0.0, NEG_INF)
            parts.append(s[c * CHUNK:(c + 1) * CHUNK] + bias)
        return jnp.concatenate(parts, axis=0)

    sa_ref[...] = scores(0)
    m0 = jnp.full((1, n2), NEG_INF, F32)
    l0 = jnp.zeros((1, n2), F32)
    m, l = lax.fori_loop(0, qi, pair, (m0, l0))
    t0 = qi * (tq // tk)
    sb_ref[...] = scores(t0 + 1)
    m, l = consume(diag_bias(sa_ref[...], 0), t0, m, l)
    m, l = consume(diag_bias(sb_ref[...], 1), t0 + 1, m, l)

    o = acc_ref[...] * (1.0 / l)
    o = o[:, :tq] - _lambda_full(lam_ref) * o[:, tq:]
    o = o * lax.rsqrt(jnp.mean(o * o, axis=0, keepdims=True) + EPS)
    o_ref[0] = (o.T * sub_ref[...] * (1.0 - LAMBDA_INIT_L1)).astype(BF16)


def _attn_prompt(qb, kb, vb, lam_p, subln):
    b, s, d = qb.shape
    tq = 512
    tk = tq // 2
    return pl.pallas_call(
        functools.partial(_attn_prompt_kernel, tq=tq, tk=tk),
        grid=(b, N_HEADS, s // tq),
        in_specs=[
            pl.BlockSpec((1, tq, HEAD_W), lambda i, h, j: (i, j, h)),
            pl.BlockSpec((1, s, HEAD_W), lambda i, h, j: (i, 0, h)),
            pl.BlockSpec((1, s, HEAD_W), lambda i, h, j: (i, 0, h)),
            pl.BlockSpec((4, HEAD_DIM), lambda i, h, j: (0, 0)),
            pl.BlockSpec((1, HEAD_W), lambda i, h, j: (0, 0)),
        ],
        out_specs=pl.BlockSpec((1, tq, HEAD_W), lambda i, h, j: (i, j, h)),
        out_shape=jax.ShapeDtypeStruct((b, s, d), BF16),
        scratch_shapes=[
            pltpu.VMEM((HEAD_W, s), BF16),
            pltpu.VMEM((HEAD_W, 2 * tq), BF16),
            pltpu.VMEM((tk, 2 * tq), F32),
            pltpu.VMEM((tk, 2 * tq), F32),
            pltpu.VMEM((HEAD_W, 2 * tq), F32),
        ],
        compiler_params=pltpu.CompilerParams(
            dimension_semantics=("arbitrary", "arbitrary", "arbitrary"), vmem_limit_bytes=VMEM_LIMIT),
        name="attn_prompt",
    )(qb, kb, vb, lam_p, subln)


def _attn_sample_kernel(q_ref, kn_ref, vn_ref, ck_ref, cv_ref, lam_ref, sub_ref, o_ref,
                        m_ref, l_ref, acc_ref, *, n):
    j = pl.program_id(1)
    last = pl.num_programs(1) - 1

    @pl.when(j == 0)
    def _():
        m_ref[...] = jnp.full(m_ref.shape, NEG_INF, F32)
        l_ref[...] = jnp.zeros(l_ref.shape, F32)
        acc_ref[...] = jnp.zeros(acc_ref.shape, F32)

    for h in range(N_HEADS):
        ln = slice(h * HEAD_W, (h + 1) * HEAD_W)
        qs = _stack_maps(q_ref[0, :, ln])
        m, l, acc = _flash_step(qs, ck_ref[0, :, ln].astype(BF16), cv_ref[0, :, ln].astype(BF16),
                                m_ref[h], l_ref[h], acc_ref[h], None)
        m_ref[h] = m
        l_ref[h] = l
        acc_ref[h] = acc

    @pl.when(j == last)
    def _():
        lam = _lambda_full(lam_ref)
        for h in range(N_HEADS):
            ln = slice(h * HEAD_W, (h + 1) * HEAD_W)
            qs = _stack_maps(q_ref[0, :, ln])
            m, l, acc = _flash_step(qs, kn_ref[0, :, ln], vn_ref[0, :, ln],
                                    m_ref[h], l_ref[h], acc_ref[h], None)
            o_ref[0, :, ln] = _diff_finish(l, acc, lam, sub_ref[...], n)


def _attn_sample(qb, kb, vb, cache_k, cache_v, lam_p, subln):
    b, n, d = qb.shape
    past = cache_k.shape[1]
    tk = min(past, 1024)
    new = pl.BlockSpec((1, n, d), lambda i, j: (i, 0, 0))
    old = pl.BlockSpec((1, tk, d), lambda i, j: (i, j, 0))
    return pl.pallas_call(
        functools.partial(_attn_sample_kernel, n=n),
        grid=(b, past // tk),
        in_specs=[new, new, new, old, old,
                  pl.BlockSpec((4, HEAD_DIM), lambda i, j: (0, 0)),
                  pl.BlockSpec((1, HEAD_W), lambda i, j: (0, 0))],
        out_specs=new,
        out_shape=jax.ShapeDtypeStruct((b, n, d), BF16),
        scratch_shapes=[
            pltpu.VMEM((N_HEADS, 2 * n, 1), F32),
            pltpu.VMEM((N_HEADS, 2 * n, 1), F32),
            pltpu.VMEM((N_HEADS, 2 * n, HEAD_W), F32),
        ],
        compiler_params=pltpu.CompilerParams(
            dimension_semantics=("arbitrary", "arbitrary"), vmem_limit_bytes=VMEM_LIMIT),
        name="attn_sample",
    )(qb, kb, vb, cache_k, cache_v, lam_p, subln)


def _outproj_kernel(x_ref, o_ref, wo_ref, y_ref):
    y_ref[...] = x_ref[...] + jnp.dot(o_ref[...], wo_ref[...], preferred_element_type=F32)


def _attn_outproj(x, ob, wo):
    t, d = x.shape
    tm = min(t, 512)
    row = pl.BlockSpec((tm, d), lambda i: (i, 0))
    return pl.pallas_call(
        _outproj_kernel,
        grid=(t // tm,),
        in_specs=[row, row, pl.BlockSpec((d, d), lambda i: (0, 0))],
        out_specs=row,
        out_shape=jax.ShapeDtypeStruct((t, d), F32),
        compiler_params=pltpu.CompilerParams(
            dimension_semantics=("arbitrary",), vmem_limit_bytes=VMEM_LIMIT),
        name="attn_outproj",
    )(x, ob, wo)


def _trunk(x, conv_state, past_k, past_v, w):
    b, s, d = x.shape
    t = b * s
    x1, new_conv = _conv_layer(x, conv_state, w["g_mix0"], w["conv_w_in"], w["conv_w"], w["conv_w_out"])
    x2 = _peer_layer(x1.reshape(t, d), w["g_ffn0"], w["g_final"], *w["peer0"], final=False)
    k, v, kb, vb, qb = _kvq_proj(x2, w["g_kv"], w["g_mix1"], w["attn_wk"], w["attn_wv"], w["attn_wq"])
    shp = (b, s, d)
    if past_k is None:
        ob = _attn_prompt(qb.reshape(shp), kb.reshape(shp), vb.reshape(shp), w["attn_lambda"], w["attn_subln"])
    else:
        ob = _attn_sample(qb.reshape(shp), kb.reshape(shp), vb.reshape(shp),
                          past_k.reshape(b, -1, d), past_v.reshape(b, -1, d),
                          w["attn_lambda"], w["attn_subln"])
    x3 = _attn_outproj(x2, ob.reshape(t, d), w["attn_wo"])
    y = _peer_layer(x3, w["g_ffn1"], w["g_final"], *w["peer1"], final=True)
    return (y.reshape(shp), k.reshape(b, s, N_HEADS, 2, HEAD_DIM), v.reshape(b, s, N_HEADS, HEAD_W),
            new_conv[None])


def kernel(x_prompt, x_sample, cache_k, cache_v, state_conv, g_mix, g_ffn, conv_w_in, conv_w, conv_w_out, g_kv, attn_wk, attn_wv, attn_wq, attn_lambda, attn_subln, attn_wo, peer_wq, peer_keys, peer_u, peer_v, g_final):
    d = D_MODEL

    def peer_weights(l):
        return (peer_wq[l].T.astype(BF16),
                peer_keys[l].reshape(2 * PEER_HEADS, N_KEYS, N_KEYS).astype(BF16),
                peer_u[l].astype(BF16),
                peer_v[l].T.astype(BF16))

    w = {
        "g_mix0": g_mix[0].reshape(1, d), "g_mix1": g_mix[1].reshape(1, d),
        "g_ffn0": g_ffn[0].reshape(1, d), "g_ffn1": g_ffn[1].reshape(1, d),
        "g_kv": g_kv.reshape(1, d), "g_final": g_final.reshape(1, d),
        "conv_w_in": conv_w_in[0].astype(BF16), "conv_w": conv_w[0], "conv_w_out": conv_w_out[0].astype(BF16),
        "attn_wk": attn_wk.astype(BF16), "attn_wv": attn_wv.astype(BF16), "attn_wq": attn_wq[0].astype(BF16),
        "attn_lambda": attn_lambda[0], "attn_subln": attn_subln[0].reshape(1, HEAD_W),
        "attn_wo": attn_wo[0].astype(BF16),
        "peer0": peer_weights(0), "peer1": peer_weights(1),
    }
    zero_state = jnp.zeros((x_prompt.shape[0], CONV_W - 1, d), F32)
    y_p, k_p, v_p, conv_p = _trunk(x_prompt, zero_state, None, None, w)
    y_s, k_s, v_s, conv_s = _trunk(x_sample, state_conv[0], cache_k, cache_v, w)
    return (y_p, y_s, k_p, v_p, conv_p, k_s, v_s, conv_s)
```

```python
import functools
import math

import jax
import jax.numpy as jnp
from jax import lax
from jax.experimental import pallas as pl
from jax.experimental.pallas import tpu as pltpu

D_MODEL = 1024
CHUNK = 64
CONV_W = 3
HEAD_DIM = 64
N_HEADS = 8
HEAD_W = 2 * HEAD_DIM
PEER_HEADS = 8
N_KEYS = 128
N_EXPERTS = N_KEYS * N_KEYS
PEER_TOPK = 16
PEER_QDIM = 256
BW_SKEW = 16
EPS = 1e-6
LAMBDA_INIT_L1 = 0.8 - 0.6 * math.exp(-0.3 * 1)
QUERY_SCALE = HEAD_DIM ** -0.5 * math.log2(math.e)

LANES = 128
VMEM_LIMIT = 56 * 1024 * 1024
TOKEN_TILE = 512
PEER_STEP_EXPERTS = 2048
ATTN_Q_TILE = 512
CACHE_KEY_TILE = 1024

F32 = jnp.float32
BF16 = jnp.bfloat16
NEG_INF = float("-inf")


def _rms_rows(x, g):
    return x * lax.rsqrt(jnp.mean(x * x, axis=-1, keepdims=True) + EPS) * g


def _conv_kernel(x_ref, st_ref, g_ref, win_ref, cw_ref, wout_ref, o_ref, ns_ref, zp_ref, *, ts):
    s = pl.program_id(1)
    d = D_MODEL

    @pl.when(s == 0)
    def _():
        zp_ref[0:8, :] = jnp.zeros((8, d), F32)
        zp_ref[6:8, :] = st_ref[0]

    x = x_ref[0]
    h = _rms_rows(x, g_ref[...]).astype(BF16)
    p = jnp.dot(h, win_ref[...], preferred_element_type=F32)
    z = p[:, d:2 * d] * p[:, 2 * d:]
    zp_ref[8:8 + ts, :] = z
    conv = (cw_ref[0:1, :] * zp_ref[6:6 + ts, :] + cw_ref[1:2, :] * zp_ref[7:7 + ts, :]
            + cw_ref[2:3, :] * z)
    y = jnp.dot((p[:, :d] * conv).astype(BF16), wout_ref[...], preferred_element_type=F32)
    o_ref[0] = x + y
    ns_ref[0] = zp_ref[ts + 6:ts + 8, :]
    zp_ref[0:8, :] = zp_ref[ts:ts + 8, :]


def _conv_layer(x, state, g, w_in, conv_w, w_out):
    b, s, d = x.shape
    ts = min(s, TOKEN_TILE)
    return pl.pallas_call(
        functools.partial(_conv_kernel, ts=ts),
        grid=(b, s // ts),
        in_specs=[
            pl.BlockSpec((1, ts, d), lambda i, j: (i, j, 0)),
            pl.BlockSpec((1, CONV_W - 1, d), lambda i, j: (i, 0, 0)),
            pl.BlockSpec((1, d), lambda i, j: (0, 0)),
            pl.BlockSpec((d, 3 * d), lambda i, j: (0, 0)),
            pl.BlockSpec((CONV_W, d), lambda i, j: (0, 0)),
            pl.BlockSpec((d, d), lambda i, j: (0, 0)),
        ],
        out_specs=[
            pl.BlockSpec((1, ts, d), lambda i, j: (i, j, 0)),
            pl.BlockSpec((1, CONV_W - 1, d), lambda i, j: (i, 0, 0)),
        ],
        out_shape=[
            jax.ShapeDtypeStruct((b, s, d), F32),
            jax.ShapeDtypeStruct((b, CONV_W - 1, d), F32),
        ],
        scratch_shapes=[pltpu.VMEM((ts + 8, d), F32)],
        compiler_params=pltpu.CompilerParams(
            dimension_semantics=("arbitrary", "arbitrary"), vmem_limit_bytes=VMEM_LIMIT),
        name="conv_layer",
    )(x, state, g, w_in, conv_w, w_out)


def _sort_network(n):
    pairs = []
    p = 1
    while p < n:
        k = p
        while k >= 1:
            for j in range(k % p, n - k, 2 * k):
                for i in range(min(k, n - j - k)):
                    if (i + j) // (2 * p) == (i + j + k) // (2 * p):
                        pairs.append((i + j, i + j + k))
            k //= 2
        p *= 2
    return pairs


def _top16(s):
    sub = s.shape[0] // PEER_TOPK
    v = [s[sub * k:sub * (k + 1)] for k in range(PEER_TOPK)]
    for i, j in _sort_network(PEER_TOPK):
        v[i], v[j] = jnp.maximum(v[i], v[j]), jnp.minimum(v[i], v[j])
    shift = sub // 2
    while shift >= 1:
        v = [jnp.maximum(v[k], pltpu.roll(v[PEER_TOPK - 1 - k], shift, 0)) for k in range(PEER_TOPK)]
        d = PEER_TOPK // 2
        while d >= 1:
            for k in range(PEER_TOPK):
                if k & d == 0:
                    v[k], v[k + d] = jnp.maximum(v[k], v[k + d]), jnp.minimum(v[k], v[k + d])
            d //= 2
        shift //= 2
    return [t[0:1] for t in v]


def _peer_route(s0, s1):
    v0 = _top16(s0)
    v1 = _top16(s1)
    rank1 = jnp.full(s1.shape, float(PEER_TOPK), F32)
    for b in reversed(range(PEER_TOPK)):
        rank1 = jnp.where(s1 >= v1[b], float(b), rank1)
    v0a = jnp.concatenate(v0, axis=0)
    v1a = jnp.concatenate(v1, axis=0)
    cand = jnp.concatenate(
        [v0[0] + v1a] + [v0[a] + v1a[0:8] for a in range(1, 8)] + [v0a[8:16] + v1[0]], axis=0)
    top = v0[0] + v1[0]
    pad = jnp.full((s0.shape[0] - cand.shape[0], cand.shape[1]), NEG_INF, F32)
    tau = _top16(jnp.concatenate([cand, pad], axis=0))[PEER_TOPK - 1]
    z = jnp.sum(jnp.where(cand >= tau, jnp.exp(cand - top), 0.0), axis=0, keepdims=True)
    cnt = jnp.zeros(s0.shape, F32)
    low = 4
    for b in range(low):
        cnt = cnt + jnp.where(s0 + v1[b] >= tau, 1.0, 0.0)
    for a in range(PEER_TOPK // (low + 1)):
        hi = jnp.zeros(tau.shape, F32)
        for b in range(low, PEER_TOPK // (a + 1)):
            hi = hi + jnp.where(v0[a] + v1[b] >= tau, 1.0, 0.0)
        cnt = cnt + jnp.where(s0 == v0[a], hi, 0.0)
    a_w = jnp.exp(s0 - v0[0]) * (0.5 / z)
    b_w = jnp.exp(s1 - v1[0])
    return cnt, rank1, a_w, b_w


def _peer_kernel(x_ref, g_ref, gf_ref, wqt_ref, keys_ref, u_ref, vt_ref, o_ref,
                 ht_ref, sc_ref, rk_ref, bw_ref, cn_ref, aw_ref, acc_ref, a_ref, w_ref,
                 *, tm, ne, final):
    e = pl.program_id(1)
    n_e = pl.num_programs(1)
    rows_per_step = ne // N_KEYS
    n_chunks = tm // LANES

    @pl.when(e == 0)
    def _prologue():
        h = _rms_rows(x_ref[...], g_ref[...])
        ht_ref[...] = h.T.astype(BF16)
        qt = jnp.dot(wqt_ref[...], ht_ref[...], preferred_element_type=F32).astype(BF16)
        for hc in range(2 * PEER_HEADS):
            sc_ref[hc] = jnp.dot(keys_ref[hc], qt[hc * N_KEYS:(hc + 1) * N_KEYS],
                                 preferred_element_type=F32)

        def route_body(idx, carry):
            hd = idx // n_chunks
            ln = pl.ds(pl.multiple_of((idx % n_chunks) * LANES, LANES), LANES)
            cnt, rank1, a_w, b_w = _peer_route(sc_ref[2 * hd, :, ln], sc_ref[2 * hd + 1, :, ln])
            cn_ref[hd, :, ln] = cnt
            aw_ref[hd, :, ln] = a_w
            rk_ref[hd, idx % n_chunks] = rank1.astype(BF16)
            bw_ref[hd, idx % n_chunks, BW_SKEW:, :] = b_w.astype(BF16)
            return carry

        lax.fori_loop(0, PEER_HEADS * n_chunks, route_body, 0)
        acc_ref[...] = jnp.zeros(acc_ref.shape, F32)

    a_ref[...] = jnp.dot(u_ref[...], ht_ref[...], preferred_element_type=F32)
    pack = (N_KEYS // 16, 16, LANES)
    for r in range(rows_per_step):
        i0 = pl.ds(pl.multiple_of(e * rows_per_step + (r // 8) * 8, 8), 8)
        rows = slice(r * N_KEYS, (r + 1) * N_KEYS)
        for tc in range(n_chunks):
            ln = slice(tc * LANES, (tc + 1) * LANES)
            gate = jnp.zeros(pack, BF16)
            for hd in range(PEER_HEADS):
                c_b = jnp.broadcast_to(cn_ref[hd, i0, ln][r % 8:r % 8 + 1], (16, LANES)).astype(BF16)
                a_b = jnp.broadcast_to(aw_ref[hd, i0, ln][r % 8:r % 8 + 1], (16, LANES)).astype(BF16)
                sel = jnp.where(rk_ref[hd, tc].reshape(pack) < c_b[None],
                                bw_ref[hd, tc, BW_SKEW:, :].reshape(pack), jnp.zeros((), BF16))
                gate = gate + sel * a_b[None]
            pre = a_ref[rows, ln]
            act = pre * (1.0 + lax.erf(pre * (2.0 ** -0.5)))
            w_ref[rows, ln] = gate.reshape(N_KEYS, LANES) * act.astype(BF16)
    acc_ref[...] += jnp.dot(vt_ref[...], w_ref[...], preferred_element_type=F32)

    @pl.when(e == n_e - 1)
    def _epilogue():
        y = x_ref[...] + acc_ref[...].T
        if final:
            y = _rms_rows(y, gf_ref[...])
        o_ref[...] = y


def _peer_layer(x, g, g_final, wq_t, keys, u, v_t, *, final):
    t, d = x.shape
    tm = min(t, TOKEN_TILE)
    ne = PEER_STEP_EXPERTS
    return pl.pallas_call(
        functools.partial(_peer_kernel, tm=tm, ne=ne, final=final),
        grid=(t // tm, N_EXPERTS // ne),
        in_specs=[
            pl.BlockSpec((tm, d), lambda i, e: (i, 0)),
            pl.BlockSpec((1, d), lambda i, e: (0, 0)),
            pl.BlockSpec((1, d), lambda i, e: (0, 0)),
            pl.BlockSpec((PEER_HEADS * PEER_QDIM, d), lambda i, e: (0, 0)),
            pl.BlockSpec((2 * PEER_HEADS, N_KEYS, N_KEYS), lambda i, e: (0, 0, 0)),
            pl.BlockSpec((ne, d), lambda i, e: (e, 0)),
            pl.BlockSpec((d, ne), lambda i, e: (0, e)),
        ],
        out_specs=pl.BlockSpec((tm, d), lambda i, e: (i, 0)),
        out_shape=jax.ShapeDtypeStruct((t, d), F32),
        scratch_shapes=[
            pltpu.VMEM((d, tm), BF16),
            pltpu.VMEM((2 * PEER_HEADS, N_KEYS, tm), F32),
            pltpu.VMEM((PEER_HEADS, tm // LANES, N_KEYS, LANES), BF16),
            pltpu.VMEM((PEER_HEADS, tm // LANES, BW_SKEW + N_KEYS, LANES), BF16),
            pltpu.VMEM((PEER_HEADS, N_KEYS, tm), F32),
            pltpu.VMEM((PEER_HEADS, N_KEYS, tm), F32),
            pltpu.VMEM((d, tm), F32),
            pltpu.VMEM((ne, tm), F32),
            pltpu.VMEM((ne, tm), BF16),
        ],
        compiler_params=pltpu.CompilerParams(
            dimension_semantics=("arbitrary", "arbitrary"), vmem_limit_bytes=VMEM_LIMIT),
        name="peer_final" if final else "peer",
    )(x, g, g_final, wq_t, keys, u, v_t)


def _kvq_kernel(x_ref, gkv_ref, gq_ref, wk_ref, wv_ref, wq_ref,
                k_ref, v_ref, kb_ref, vb_ref, qb_ref):
    x = x_ref[...]
    n = x * lax.rsqrt(jnp.mean(x * x, axis=-1, keepdims=True) + EPS)
    hk = (n * gkv_ref[...]).astype(BF16)
    hq = (n * gq_ref[...]).astype(BF16)
    k = jnp.dot(hk, wk_ref[...], preferred_element_type=F32)
    v = jnp.dot(hk, wv_ref[...], preferred_element_type=F32)
    q = jnp.dot(hq, wq_ref[...], preferred_element_type=F32)
    k_ref[...] = k
    v_ref[...] = v
    kb_ref[...] = k.astype(BF16)
    vb_ref[...] = v.astype(BF16)
    qb_ref[...] = (q * QUERY_SCALE).astype(BF16)


def _kvq_proj(x, g_kv, g_q, wk, wv, wq):
    t, d = x.shape
    tm = min(t, TOKEN_TILE)
    row = pl.BlockSpec((tm, d), lambda i: (i, 0))
    vec = pl.BlockSpec((1, d), lambda i: (0, 0))
    mat = pl.BlockSpec((d, d), lambda i: (0, 0))
    return pl.pallas_call(
        _kvq_kernel,
        grid=(t // tm,),
        in_specs=[row, vec, vec, mat, mat, mat],
        out_specs=[row] * 5,
        out_shape=[jax.ShapeDtypeStruct((t, d), F32)] * 2 + [jax.ShapeDtypeStruct((t, d), BF16)] * 3,
        compiler_params=pltpu.CompilerParams(
            dimension_semantics=("arbitrary",), vmem_limit_bytes=VMEM_LIMIT),
        name="kvq_proj",
    )(x, g_kv, g_q, wk, wv, wq)


def _lambda_full(lam_ref):
    lp = lam_ref[...]
    s01 = jnp.sum(lp[0:1] * lp[1:2], axis=-1, keepdims=True)
    s23 = jnp.sum(lp[2:3] * lp[3:4], axis=-1, keepdims=True)
    return jnp.exp(s01) - jnp.exp(s23) + LAMBDA_INIT_L1


def _stack_maps(q):
    lane = lax.broadcasted_iota(jnp.int32, q.shape, 1)
    zero = jnp.zeros((), q.dtype)
    return jnp.concatenate([jnp.where(lane < HEAD_DIM, q, zero), jnp.where(lane >= HEAD_DIM, q, zero)],
                           axis=0)


def _flash_step(qs, k, v, m, l, acc, mask):
    s = lax.dot_general(qs, k, (((1,), (1,)), ((), ())), preferred_element_type=F32)
    if mask is not None:
        s = jnp.where(mask, s, NEG_INF)
    m_new = jnp.maximum(m, jnp.max(s, axis=-1, keepdims=True))
    alpha = jnp.exp2(m - m_new)
    p = jnp.exp2(s - m_new)
    l = alpha * l + jnp.sum(p, axis=-1, keepdims=True)
    acc = alpha * acc + jnp.dot(p.astype(BF16), v, preferred_element_type=F32)
    return m_new, l, acc


def _diff_finish(l, acc, lam, subln, n):
    o = acc / l
    o = o[:n] - lam * o[n:]
    o = o * lax.rsqrt(jnp.mean(o * o, axis=-1, keepdims=True) + EPS) * subln
    return (o * (1.0 - LAMBDA_INIT_L1)).astype(BF16)


def _attn_prompt_kernel(q_ref, k_ref, v_ref, lam_ref, sub_ref, o_ref,
                        vt_ref, qt_ref, sa_ref, sb_ref, acc_ref, *, tq, tk):
    qi = pl.program_id(2)
    n2 = 2 * tq
    seq = v_ref.shape[1]

    @pl.when(qi == 0)
    def _():
        for c in range(seq // tq):
            rows = slice(c * tq, (c + 1) * tq)
            vt_ref[:, rows] = v_ref[0, rows, :].astype(F32).T.astype(BF16)

    qt_ref[...] = _stack_maps(q_ref[0]).astype(F32).T.astype(BF16)
    acc_ref[...] = jnp.zeros(acc_ref.shape, F32)

    def scores(t):
        rows = pl.ds(pl.multiple_of(t * tk, tk), tk)
        return jnp.dot(k_ref[0, rows, :], qt_ref[...], preferred_element_type=F32)

    def consume(s, t, m, l):
        cols = pl.ds(pl.multiple_of(t * tk, tk), tk)
        m_new = jnp.maximum(m, jnp.max(s, axis=0, keepdims=True))
        alpha = jnp.exp2(m - m_new)
        p = jnp.exp2(s - m_new)
        l = alpha * l + jnp.sum(p, axis=0, keepdims=True)
        acc_ref[...] = alpha * acc_ref[...] + jnp.dot(vt_ref[:, cols], p.astype(BF16),
                                                      preferred_element_type=F32)
        return m_new, l

    def pair(p, carry):
        m, l = carry
        sb_ref[...] = scores(2 * p + 1)
        m, l = consume(sa_ref[...], 2 * p, m, l)
        sa_ref[...] = scores(2 * p + 2)
        return consume(sb_ref[...], 2 * p + 1, m, l)

    def diag_bias(s, half):
        qc = (lax.broadcasted_iota(jnp.int32, (1, n2), 1) % tq) // CHUNK
        parts = []
        for c in range(tk // CHUNK):
            kc = half * (tk // CHUNK) + c
            bias = jnp.where(qc >= kc, 0.0, NEG_INF)
            parts.append(s[c * CHUNK:(c + 1) * CHUNK] + bias)
        return jnp.concatenate(parts, axis=0)

    sa_ref[...] = scores(0)
    m0 = jnp.full((1, n2), NEG_INF, F32)
    l0 = jnp.zeros((1, n2), F32)
    m, l = lax.fori_loop(0, qi, pair, (m0, l0))
    t0 = qi * (tq // tk)
    sb_ref[...] = scores(t0 + 1)
    m, l = consume(diag_bias(sa_ref[...], 0), t0, m, l)
    m, l = consume(diag_bias(sb_ref[...], 1), t0 + 1, m, l)

    o = acc_ref[...] * (1.0 / l)
    o = o[:, :tq] - _lambda_full(lam_ref) * o[:, tq:]
    o = o * lax.rsqrt(jnp.mean(o * o, axis=0, keepdims=True) + EPS)
    o_ref[0] = (o.T * sub_ref[...] * (1.0 - LAMBDA_INIT_L1)).astype(BF16)


def _attn_prompt(qb, kb, vb, lam_p, subln):
    b, s, d = qb.shape
    tq = ATTN_Q_TILE
    tk = tq // 2
    return pl.pallas_call(
        functools.partial(_attn_prompt_kernel, tq=tq, tk=tk),
        grid=(b, N_HEADS, s // tq),
        in_specs=[
            pl.BlockSpec((1, tq, HEAD_W), lambda i, h, j: (i, j, h)),
            pl.BlockSpec((1, s, HEAD_W), lambda i, h, j: (i, 0, h)),
            pl.BlockSpec((1, s, HEAD_W), lambda i, h, j: (i, 0, h)),
            pl.BlockSpec((4, HEAD_DIM), lambda i, h, j: (0, 0)),
            pl.BlockSpec((1, HEAD_W), lambda i, h, j: (0, 0)),
        ],
        out_specs=pl.BlockSpec((1, tq, HEAD_W), lambda i, h, j: (i, j, h)),
        out_shape=jax.ShapeDtypeStruct((b, s, d), BF16),
        scratch_shapes=[
            pltpu.VMEM((HEAD_W, s), BF16),
            pltpu.VMEM((HEAD_W, 2 * tq), BF16),
            pltpu.VMEM((tk, 2 * tq), F32),
            pltpu.VMEM((tk, 2 * tq), F32),
            pltpu.VMEM((HEAD_W, 2 * tq), F32),
        ],
        compiler_params=pltpu.CompilerParams(
            dimension_semantics=("arbitrary", "arbitrary", "arbitrary"), vmem_limit_bytes=VMEM_LIMIT),
        name="attn_prompt",
    )(qb, kb, vb, lam_p, subln)


def _attn_sample_kernel(q_ref, kn_ref, vn_ref, ck_ref, cv_ref, lam_ref, sub_ref, o_ref,
                        m_ref, l_ref, acc_ref, *, n):
    j = pl.program_id(1)
    last = pl.num_programs(1) - 1

    @pl.when(j == 0)
    def _():
        m_ref[...] = jnp.full(m_ref.shape, NEG_INF, F32)
        l_ref[...] = jnp.zeros(l_ref.shape, F32)
        acc_ref[...] = jnp.zeros(acc_ref.shape, F32)

    for h in range(N_HEADS):
        ln = slice(h * HEAD_W, (h + 1) * HEAD_W)
        qs = _stack_maps(q_ref[0, :, ln])
        m, l, acc = _flash_step(qs, ck_ref[0, :, ln].astype(BF16), cv_ref[0, :, ln].astype(BF16),
                                m_ref[h], l_ref[h], acc_ref[h], None)
        m_ref[h] = m
        l_ref[h] = l
        acc_ref[h] = acc

    @pl.when(j == last)
    def _():
        lam = _lambda_full(lam_ref)
        for h in range(N_HEADS):
            ln = slice(h * HEAD_W, (h + 1) * HEAD_W)
            qs = _stack_maps(q_ref[0, :, ln])
            m, l, acc = _flash_step(qs, kn_ref[0, :, ln], vn_ref[0, :, ln],
                                    m_ref[h], l_ref[h], acc_ref[h], None)
            o_ref[0, :, ln] = _diff_finish(l, acc, lam, sub_ref[...], n)


def _attn_sample(qb, kb, vb, cache_k, cache_v, lam_p, subln):
    b, n, d = qb.shape
    past = cache_k.shape[1]
    tk = min(past, CACHE_KEY_TILE)
    new = pl.BlockSpec((1, n, d), lambda i, j: (i, 0, 0))
    old = pl.BlockSpec((1, tk, d), lambda i, j: (i, j, 0))
    return pl.pallas_call(
        functools.partial(_attn_sample_kernel, n=n),
        grid=(b, past // tk),
        in_specs=[new, new, new, old, old,
                  pl.BlockSpec((4, HEAD_DIM), lambda i, j: (0, 0)),
                  pl.BlockSpec((1, HEAD_W), lambda i, j: (0, 0))],
        out_specs=new,
        out_shape=jax.ShapeDtypeStruct((b, n, d), BF16),
        scratch_shapes=[
            pltpu.VMEM((N_HEADS, 2 * n, 1), F32),
            pltpu.VMEM((N_HEADS, 2 * n, 1), F32),
            pltpu.VMEM((N_HEADS, 2 * n, HEAD_W), F32),
        ],
        compiler_params=pltpu.CompilerParams(
            dimension_semantics=("arbitrary", "arbitrary"), vmem_limit_bytes=VMEM_LIMIT),
        name="attn_sample",
    )(qb, kb, vb, cache_k, cache_v, lam_p, subln)


def _outproj_kernel(x_ref, o_ref, wo_ref, y_ref):
    y_ref[...] = x_ref[...] + jnp.dot(o_ref[...], wo_ref[...], preferred_element_type=F32)


def _attn_outproj(x, ob, wo):
    t, d = x.shape
    tm = min(t, TOKEN_TILE)
    row = pl.BlockSpec((tm, d), lambda i: (i, 0))
    return pl.pallas_call(
        _outproj_kernel,
        grid=(t // tm,),
        in_specs=[row, row, pl.BlockSpec((d, d), lambda i: (0, 0))],
        out_specs=row,
        out_shape=jax.ShapeDtypeStruct((t, d), F32),
        compiler_params=pltpu.CompilerParams(
            dimension_semantics=("arbitrary",), vmem_limit_bytes=VMEM_LIMIT),
        name="attn_outproj",
    )(x, ob, wo)


def _trunk(x, conv_state, past_k, past_v, w):
    b, s, d = x.shape
    t = b * s
    x1, new_conv = _conv_layer(x, conv_state, w["g_mix0"], w["conv_w_in"], w["conv_w"], w["conv_w_out"])
    x2 = _peer_layer(x1.reshape(t, d), w["g_ffn0"], w["g_final"], *w["peer0"], final=False)
    k, v, kb, vb, qb = _kvq_proj(x2, w["g_kv"], w["g_mix1"], w["attn_wk"], w["attn_wv"], w["attn_wq"])
    shp = (b, s, d)
    if past_k is None:
        ob = _attn_prompt(qb.reshape(shp), kb.reshape(shp), vb.reshape(shp), w["attn_lambda"], w["attn_subln"])
    else:
        ob = _attn_sample(qb.reshape(shp), kb.reshape(shp), vb.reshape(shp),
                          past_k.reshape(b, -1, d), past_v.reshape(b, -1, d),
                          w["attn_lambda"], w["attn_subln"])
    x3 = _attn_outproj(x2, ob.reshape(t, d), w["attn_wo"])
    y = _peer_layer(x3, w["g_ffn1"], w["g_final"], *w["peer1"], final=True)
    return (y.reshape(shp), k.reshape(b, s, N_HEADS, 2, HEAD_DIM), v.reshape(b, s, N_HEADS, HEAD_W),
            new_conv[None])


def kernel(x_prompt, x_sample, cache_k, cache_v, state_conv, g_mix, g_ffn, conv_w_in, conv_w, conv_w_out, g_kv, attn_wk, attn_wv, attn_wq, attn_lambda, attn_subln, attn_wo, peer_wq, peer_keys, peer_u, peer_v, g_final):
    d = D_MODEL

    def peer_weights(l):
        return (peer_wq[l].T.astype(BF16),
                peer_keys[l].reshape(2 * PEER_HEADS, N_KEYS, N_KEYS).astype(BF16),
                peer_u[l].astype(BF16),
                peer_v[l].T.astype(BF16))

    w = {
        "g_mix0": g_mix[0].reshape(1, d), "g_mix1": g_mix[1].reshape(1, d),
        "g_ffn0": g_ffn[0].reshape(1, d), "g_ffn1": g_ffn[1].reshape(1, d),
        "g_kv": g_kv.reshape(1, d), "g_final": g_final.reshape(1, d),
        "conv_w_in": conv_w_in[0].astype(BF16), "conv_w": conv_w[0], "conv_w_out": conv_w_out[0].astype(BF16),
        "attn_wk": attn_wk.astype(BF16), "attn_wv": attn_wv.astype(BF16), "attn_wq": attn_wq[0].astype(BF16),
        "attn_lambda": attn_lambda[0], "attn_subln": attn_subln[0].reshape(1, HEAD_W),
        "attn_wo": attn_wo[0].astype(BF16),
        "peer0": peer_weights(0), "peer1": peer_weights(1),
    }
    zero_state = jnp.zeros((x_prompt.shape[0], CONV_W - 1, d), F32)
    y_p, k_p, v_p, conv_p = _trunk(x_prompt, zero_state, None, None, w)
    y_s, k_s, v_s, conv_s = _trunk(x_sample, state_conv[0], cache_k, cache_v, w)
    return (y_p, y_s, k_p, v_p, conv_p, k_s, v_s, conv_s)
```

```python
import functools
import math

import jax
import jax.numpy as jnp
from jax import lax
from jax.experimental import pallas as pl
from jax.experimental.pallas import tpu as pltpu

D_MODEL = 1024
CHUNK = 64
CONV_W = 3
HEAD_DIM = 64
N_HEADS = 8
HEAD_W = 2 * HEAD_DIM
PEER_HEADS = 8
N_KEYS = 128
N_EXPERTS = N_KEYS * N_KEYS
PEER_TOPK = 16
PEER_QDIM = 256
EPS = 1e-6
LAMBDA_INIT_L1 = 0.8 - 0.6 * math.exp(-0.3 * 1)
QUERY_SCALE = HEAD_DIM ** -0.5 * math.log2(math.e)

LANES = 128
VMEM_LIMIT = 56 * 1024 * 1024
TOKEN_TILE = 512
PEER_STEP_EXPERTS = 2048
ATTN_Q_TILE = 512
CACHE_KEY_TILE = 1024

F32 = jnp.float32
BF16 = jnp.bfloat16
NEG_INF = float("-inf")


def _rms_rows(x, g):
    return x * lax.rsqrt(jnp.mean(x * x, axis=-1, keepdims=True) + EPS) * g


def _conv_kernel(x_ref, st_ref, g_ref, win_ref, cw_ref, wout_ref, o_ref, ns_ref, zp_ref, *, ts):
    s = pl.program_id(1)
    d = D_MODEL

    @pl.when(s == 0)
    def _():
        zp_ref[0:8, :] = jnp.zeros((8, d), F32)
        zp_ref[6:8, :] = st_ref[0]

    x = x_ref[0]
    h = _rms_rows(x, g_ref[...]).astype(BF16)
    p = jnp.dot(h, win_ref[...], preferred_element_type=F32)
    z = p[:, d:2 * d] * p[:, 2 * d:]
    zp_ref[8:8 + ts, :] = z
    conv = (cw_ref[0:1, :] * zp_ref[6:6 + ts, :] + cw_ref[1:2, :] * zp_ref[7:7 + ts, :]
            + cw_ref[2:3, :] * z)
    y = jnp.dot((p[:, :d] * conv).astype(BF16), wout_ref[...], preferred_element_type=F32)
    o_ref[0] = x + y
    ns_ref[0] = zp_ref[ts + 6:ts + 8, :]
    zp_ref[0:8, :] = zp_ref[ts:ts + 8, :]


def _conv_layer(x, state, g, w_in, conv_w, w_out):
    b, s, d = x.shape
    ts = min(s, TOKEN_TILE)
    return pl.pallas_call(
        functools.partial(_conv_kernel, ts=ts),
        grid=(b, s // ts),
        in_specs=[
            pl.BlockSpec((1, ts, d), lambda i, j: (i, j, 0)),
            pl.BlockSpec((1, CONV_W - 1, d), lambda i, j: (i, 0, 0)),
            pl.BlockSpec((1, d), lambda i, j: (0, 0)),
            pl.BlockSpec((d, 3 * d), lambda i, j: (0, 0)),
            pl.BlockSpec((CONV_W, d), lambda i, j: (0, 0)),
            pl.BlockSpec((d, d), lambda i, j: (0, 0)),
        ],
        out_specs=[
            pl.BlockSpec((1, ts, d), lambda i, j: (i, j, 0)),
            pl.BlockSpec((1, CONV_W - 1, d), lambda i, j: (i, 0, 0)),
        ],
        out_shape=[
            jax.ShapeDtypeStruct((b, s, d), F32),
            jax.ShapeDtypeStruct((b, CONV_W - 1, d), F32),
        ],
        scratch_shapes=[pltpu.VMEM((ts + 8, d), F32)],
        compiler_params=pltpu.CompilerParams(
            dimension_semantics=("arbitrary", "arbitrary"), vmem_limit_bytes=VMEM_LIMIT),
        name="conv_layer",
    )(x, state, g, w_in, conv_w, w_out)


def _sort_network(n):
    pairs = []
    p = 1
    while p < n:
        k = p
        while k >= 1:
            for j in range(k % p, n - k, 2 * k):
                for i in range(min(k, n - j - k)):
                    if (i + j) // (2 * p) == (i + j + k) // (2 * p):
                        pairs.append((i + j, i + j + k))
            k //= 2
        p *= 2
    return pairs


def _top16(s):
    sub = s.shape[0] // PEER_TOPK
    v = [s[sub * k:sub * (k + 1)] for k in range(PEER_TOPK)]
    for i, j in _sort_network(PEER_TOPK):
        v[i], v[j] = jnp.maximum(v[i], v[j]), jnp.minimum(v[i], v[j])
    shift = sub // 2
    while shift >= 1:
        v = [jnp.maximum(v[k], pltpu.roll(v[PEER_TOPK - 1 - k], shift, 0)) for k in range(PEER_TOPK)]
        d = PEER_TOPK // 2
        while d >= 1:
            for k in range(PEER_TOPK):
                if k & d == 0:
                    v[k], v[k + d] = jnp.maximum(v[k], v[k + d]), jnp.minimum(v[k], v[k + d])
            d //= 2
        shift //= 2
    return [t[0:1] for t in v]


def _peer_route(s0, s1):
    v0 = _top16(s0)
    v1 = _top16(s1)
    rank1 = jnp.full(s1.shape, float(PEER_TOPK), F32)
    for b in reversed(range(PEER_TOPK)):
        rank1 = jnp.where(s1 >= v1[b], float(b), rank1)
    v0a = jnp.concatenate(v0, axis=0)
    v1a = jnp.concatenate(v1, axis=0)
    cand = jnp.concatenate(
        [v0[0] + v1a] + [v0[a] + v1a[0:8] for a in range(1, 8)] + [v0a[8:16] + v1[0]], axis=0)
    top = v0[0] + v1[0]
    pad = jnp.full((s0.shape[0] - cand.shape[0], cand.shape[1]), NEG_INF, F32)
    tau = _top16(jnp.concatenate([cand, pad], axis=0))[PEER_TOPK - 1]
    z = jnp.sum(jnp.where(cand >= tau, jnp.exp(cand - top), 0.0), axis=0, keepdims=True)
    cnt = jnp.zeros(s0.shape, F32)
    low = 4
    for b in range(low):
        cnt = cnt + jnp.where(s0 + v1[b] >= tau, 1.0, 0.0)
    for a in range(PEER_TOPK // (low + 1)):
        hi = jnp.zeros(tau.shape, F32)
        for b in range(low, PEER_TOPK // (a + 1)):
            hi = hi + jnp.where(v0[a] + v1[b] >= tau, 1.0, 0.0)
        cnt = cnt + jnp.where(s0 == v0[a], hi, 0.0)
    a_w = jnp.exp(s0 - v0[0]) * (0.5 / z)
    b_w = jnp.exp(s1 - v1[0])
    return cnt, rank1, a_w, b_w


def _peer_kernel(x_ref, g_ref, gf_ref, wqt_ref, keys_ref, u_ref, vt_ref, o_ref,
                 ht_ref, sc_ref, rk_ref, bw_ref, cn_ref, aw_ref, acc_ref, a_ref, w_ref,
                 *, tm, ne, final):
    e = pl.program_id(1)
    n_e = pl.num_programs(1)
    rows_per_step = ne // N_KEYS
    n_chunks = tm // LANES

    @pl.when(e == 0)
    def _prologue():
        h = _rms_rows(x_ref[...], g_ref[...])
        ht_ref[...] = h.T.astype(BF16)
        qt = jnp.dot(wqt_ref[...], ht_ref[...], preferred_element_type=F32).astype(BF16)
        for hc in range(2 * PEER_HEADS):
            sc_ref[hc] = jnp.dot(keys_ref[hc], qt[hc * N_KEYS:(hc + 1) * N_KEYS],
                                 preferred_element_type=F32)

        def route_body(idx, carry):
            hd = idx // n_chunks
            ln = pl.ds(pl.multiple_of((idx % n_chunks) * LANES, LANES), LANES)
            cnt, rank1, a_w, b_w = _peer_route(sc_ref[2 * hd, :, ln], sc_ref[2 * hd + 1, :, ln])
            cn_ref[hd, :, ln] = cnt
            aw_ref[hd, :, ln] = a_w
            rk_ref[hd, idx % n_chunks] = rank1.astype(BF16)
            bw_ref[hd, idx % n_chunks] = b_w.astype(BF16)
            return carry

        lax.fori_loop(0, PEER_HEADS * n_chunks, route_body, 0)
        acc_ref[...] = jnp.zeros(acc_ref.shape, F32)

    a_ref[...] = jnp.dot(u_ref[...], ht_ref[...], preferred_element_type=F32)
    pack = (N_KEYS // 16, 16, LANES)
    for r in range(rows_per_step):
        i0 = pl.ds(pl.multiple_of(e * rows_per_step + (r // 8) * 8, 8), 8)
        rows = slice(r * N_KEYS, (r + 1) * N_KEYS)
        for tc in range(n_chunks):
            ln = slice(tc * LANES, (tc + 1) * LANES)
            gate = jnp.zeros(pack, BF16)
            for hd in range(PEER_HEADS):
                c_b = jnp.broadcast_to(cn_ref[hd, i0, ln][r % 8:r % 8 + 1], (16, LANES)).astype(BF16)
                a_b = jnp.broadcast_to(aw_ref[hd, i0, ln][r % 8:r % 8 + 1], (16, LANES)).astype(BF16)
                sel = jnp.where(rk_ref[hd, tc].reshape(pack) < c_b[None],
                                bw_ref[hd, tc].reshape(pack), jnp.zeros((), BF16))
                gate = gate + sel * a_b[None]
            pre = a_ref[rows, ln]
            act = pre * (1.0 + lax.erf(pre * (2.0 ** -0.5)))
            w_ref[rows, ln] = gate.reshape(N_KEYS, LANES) * act.astype(BF16)
    acc_ref[...] += jnp.dot(vt_ref[...], w_ref[...], preferred_element_type=F32)

    @pl.when(e == n_e - 1)
    def _epilogue():
        y = x_ref[...] + acc_ref[...].T
        if final:
            y = _rms_rows(y, gf_ref[...])
        o_ref[...] = y


def _peer_layer(x, g, g_final, wq_t, keys, u, v_t, *, final):
    t, d = x.shape
    tm = min(t, TOKEN_TILE)
    ne = PEER_STEP_EXPERTS
    return pl.pallas_call(
        functools.partial(_peer_kernel, tm=tm, ne=ne, final=final),
        grid=(t // tm, N_EXPERTS // ne),
        in_specs=[
            pl.BlockSpec((tm, d), lambda i, e: (i, 0)),
            pl.BlockSpec((1, d), lambda i, e: (0, 0)),
            pl.BlockSpec((1, d), lambda i, e: (0, 0)),
            pl.BlockSpec((PEER_HEADS * PEER_QDIM, d), lambda i, e: (0, 0)),
            pl.BlockSpec((2 * PEER_HEADS, N_KEYS, N_KEYS), lambda i, e: (0, 0, 0)),
            pl.BlockSpec((ne, d), lambda i, e: (e, 0)),
            pl.BlockSpec((d, ne), lambda i, e: (0, e)),
        ],
        out_specs=pl.BlockSpec((tm, d), lambda i, e: (i, 0)),
        out_shape=jax.ShapeDtypeStruct((t, d), F32),
        scratch_shapes=[
            pltpu.VMEM((d, tm), BF16),
            pltpu.VMEM((2 * PEER_HEADS, N_KEYS, tm), F32),
            pltpu.VMEM((PEER_HEADS, tm // LANES, N_KEYS, LANES), BF16),
            pltpu.VMEM((PEER_HEADS, tm // LANES, N_KEYS, LANES), BF16),
            pltpu.VMEM((PEER_HEADS, N_KEYS, tm), F32),
            pltpu.VMEM((PEER_HEADS, N_KEYS, tm), F32),
            pltpu.VMEM((d, tm), F32),
            pltpu.VMEM((ne, tm), F32),
            pltpu.VMEM((ne, tm), BF16),
        ],
        compiler_params=pltpu.CompilerParams(
            dimension_semantics=("arbitrary", "arbitrary"), vmem_limit_bytes=VMEM_LIMIT),
        name="peer_final" if final else "peer",
    )(x, g, g_final, wq_t, keys, u, v_t)


def _kvq_kernel(x_ref, gkv_ref, gq_ref, wk_ref, wv_ref, wq_ref,
                k_ref, v_ref, kb_ref, vb_ref, qb_ref):
    x = x_ref[...]
    n = x * lax.rsqrt(jnp.mean(x * x, axis=-1, keepdims=True) + EPS)
    hk = (n * gkv_ref[...]).astype(BF16)
    hq = (n * gq_ref[...]).astype(BF16)
    k = jnp.dot(hk, wk_ref[...], preferred_element_type=F32)
    v = jnp.dot(hk, wv_ref[...], preferred_element_type=F32)
    q = jnp.dot(hq, wq_ref[...], preferred_element_type=F32)
    k_ref[...] = k
    v_ref[...] = v
    kb_ref[...] = k.astype(BF16)
    vb_ref[...] = v.astype(BF16)
    qb_ref[...] = (q * QUERY_SCALE).astype(BF16)


def _kvq_proj(x, g_kv, g_q, wk, wv, wq):
    t, d = x.shape
    tm = min(t, TOKEN_TILE)
    row = pl.BlockSpec((tm, d), lambda i: (i, 0))
    vec = pl.BlockSpec((1, d), lambda i: (0, 0))
    mat = pl.BlockSpec((d, d), lambda i: (0, 0))
    return pl.pallas_call(
        _kvq_kernel,
        grid=(t // tm,),
        in_specs=[row, vec, vec, mat, mat, mat],
        out_specs=[row] * 5,
        out_shape=[jax.ShapeDtypeStruct((t, d), F32)] * 2 + [jax.ShapeDtypeStruct((t, d), BF16)] * 3,
        compiler_params=pltpu.CompilerParams(
            dimension_semantics=("arbitrary",), vmem_limit_bytes=VMEM_LIMIT),
        name="kvq_proj",
    )(x, g_kv, g_q, wk, wv, wq)


def _lambda_full(lam_ref):
    lp = lam_ref[...]
    s01 = jnp.sum(lp[0:1] * lp[1:2], axis=-1, keepdims=True)
    s23 = jnp.sum(lp[2:3] * lp[3:4], axis=-1, keepdims=True)
    return jnp.exp(s01) - jnp.exp(s23) + LAMBDA_INIT_L1


def _stack_maps(q):
    lane = lax.broadcasted_iota(jnp.int32, q.shape, 1)
    zero = jnp.zeros((), q.dtype)
    return jnp.concatenate([jnp.where(lane < HEAD_DIM, q, zero), jnp.where(lane >= HEAD_DIM, q, zero)],
                           axis=0)


def _flash_step(qs, k, v, m, l, acc, mask):
    s = lax.dot_general(qs, k, (((1,), (1,)), ((), ())), preferred_element_type=F32)
    if mask is not None:
        s = jnp.where(mask, s, NEG_INF)
    m_new = jnp.maximum(m, jnp.max(s, axis=-1, keepdims=True))
    alpha = jnp.exp2(m - m_new)
    p = jnp.exp2(s - m_new)
    l = alpha * l + jnp.sum(p, axis=-1, keepdims=True)
    acc = alpha * acc + jnp.dot(p.astype(BF16), v, preferred_element_type=F32)
    return m_new, l, acc


def _diff_finish(l, acc, lam, subln, n):
    o = acc / l
    o = o[:n] - lam * o[n:]
    o = o * lax.rsqrt(jnp.mean(o * o, axis=-1, keepdims=True) + EPS) * subln
    return (o * (1.0 - LAMBDA_INIT_L1)).astype(BF16)


def _attn_prompt_kernel(q_ref, k_ref, v_ref, lam_ref, sub_ref, o_ref,
                        vt_ref, qt_ref, sa_ref, sb_ref, acc_ref, *, tq, tk):
    qi = pl.program_id(2)
    n2 = 2 * tq
    seq = v_ref.shape[1]

    @pl.when(qi == 0)
    def _():
        for c in range(seq // tq):
            rows = slice(c * tq, (c + 1) * tq)
            vt_ref[:, rows] = v_ref[0, rows, :].astype(F32).T.astype(BF16)

    qt_ref[...] = _stack_maps(q_ref[0]).astype(F32).T.astype(BF16)
    acc_ref[...] = jnp.zeros(acc_ref.shape, F32)

    def scores(t):
        rows = pl.ds(pl.multiple_of(t * tk, tk), tk)
        return jnp.dot(k_ref[0, rows, :], qt_ref[...], preferred_element_type=F32)

    def consume(s, t, m, l):
        cols = pl.ds(pl.multiple_of(t * tk, tk), tk)
        m_new = jnp.maximum(m, jnp.max(s, axis=0, keepdims=True))
        alpha = jnp.exp2(m - m_new)
        p = jnp.exp2(s - m_new)
        l = alpha * l + jnp.sum(p, axis=0, keepdims=True)
        acc_ref[...] = alpha * acc_ref[...] + jnp.dot(vt_ref[:, cols], p.astype(BF16),
                                                      preferred_element_type=F32)
        return m_new, l

    def pair(p, carry):
        m, l = carry
        sb_ref[...] = scores(2 * p + 1)
        m, l = consume(sa_ref[...], 2 * p, m, l)
        sa_ref[...] = scores(2 * p + 2)
        return consume(sb_ref[...], 2 * p + 1, m, l)

    def diag_bias(s, half):
        qc = (lax.broadcasted_iota(jnp.int32, (1, n2), 1) % tq) // CHUNK
        parts = []
        for c in range(tk // CHUNK):
            kc = half * (tk // CHUNK) + c
            bias = jnp.where(qc >= kc, 0.0, NEG_INF)
            parts.append(s[c * CHUNK:(c + 1) * CHUNK] + bias)
        return jnp.concatenate(parts, axis=0)

    sa_ref[...] = scores(0)
    m0 = jnp.full((1, n2), NEG_INF, F32)
    l0 = jnp.zeros((1, n2), F32)
    m, l = lax.fori_loop(0, qi, pair, (m0, l0))
    t0 = qi * (tq // tk)
    sb_ref[...] = scores(t0 + 1)
    m, l = consume(diag_bias(sa_ref[...], 0), t0, m, l)
    m, l = consume(diag_bias(sb_ref[...], 1), t0 + 1, m, l)

    o = acc_ref[...] * (1.0 / l)
    o = o[:, :tq] - _lambda_full(lam_ref) * o[:, tq:]
    o = o * lax.rsqrt(jnp.mean(o * o, axis=0, keepdims=True) + EPS)
    o_ref[0] = (o.T * sub_ref[...] * (1.0 - LAMBDA_INIT_L1)).astype(BF16)


def _attn_prompt(qb, kb, vb, lam_p, subln):
    b, s, d = qb.shape
    tq = ATTN_Q_TILE
    tk = tq // 2
    return pl.pallas_call(
        functools.partial(_attn_prompt_kernel, tq=tq, tk=tk),
        grid=(b, N_HEADS, s // tq),
        in_specs=[
            pl.BlockSpec((1, tq, HEAD_W), lambda i, h, j: (i, j, h)),
            pl.BlockSpec((1, s, HEAD_W), lambda i, h, j: (i, 0, h)),
            pl.BlockSpec((1, s, HEAD_W), lambda i, h, j: (i, 0, h)),
            pl.BlockSpec((4, HEAD_DIM), lambda i, h, j: (0, 0)),
            pl.BlockSpec((1, HEAD_W), lambda i, h, j: (0, 0)),
        ],
        out_specs=pl.BlockSpec((1, tq, HEAD_W), lambda i, h, j: (i, j, h)),
        out_shape=jax.ShapeDtypeStruct((b, s, d), BF16),
        scratch_shapes=[
            pltpu.VMEM((HEAD_W, s), BF16),
            pltpu.VMEM((HEAD_W, 2 * tq), BF16),
            pltpu.VMEM((tk, 2 * tq), F32),
            pltpu.VMEM((tk, 2 * tq), F32),
            pltpu.VMEM((HEAD_W, 2 * tq), F32),
        ],
        compiler_params=pltpu.CompilerParams(
            dimension_semantics=("arbitrary", "arbitrary", "arbitrary"), vmem_limit_bytes=VMEM_LIMIT),
        name="attn_prompt",
    )(qb, kb, vb, lam_p, subln)


def _attn_sample_kernel(q_ref, kn_ref, vn_ref, ck_ref, cv_ref, lam_ref, sub_ref, o_ref,
                        m_ref, l_ref, acc_ref, *, n):
    j = pl.program_id(1)
    last = pl.num_programs(1) - 1

    @pl.when(j == 0)
    def _():
        m_ref[...] = jnp.full(m_ref.shape, NEG_INF, F32)
        l_ref[...] = jnp.zeros(l_ref.shape, F32)
        acc_ref[...] = jnp.zeros(acc_ref.shape, F32)

    for h in range(N_HEADS):
        ln = slice(h * HEAD_W, (h + 1) * HEAD_W)
        qs = _stack_maps(q_ref[0, :, ln])
        m, l, acc = _flash_step(qs, ck_ref[0, :, ln].astype(BF16), cv_ref[0, :, ln].astype(BF16),
                                m_ref[h], l_ref[h], acc_ref[h], None)
        m_ref[h] = m
        l_ref[h] = l
        acc_ref[h] = acc

    @pl.when(j == last)
    def _():
        lam = _lambda_full(lam_ref)
        for h in range(N_HEADS):
            ln = slice(h * HEAD_W, (h + 1) * HEAD_W)
            qs = _stack_maps(q_ref[0, :, ln])
            m, l, acc = _flash_step(qs, kn_ref[0, :, ln], vn_ref[0, :, ln],
                                    m_ref[h], l_ref[h], acc_ref[h], None)
            o_ref[0, :, ln] = _diff_finish(l, acc, lam, sub_ref[...], n)


def _attn_sample(qb, kb, vb, cache_k, cache_v, lam_p, subln):
    b, n, d = qb.shape
    past = cache_k.shape[1]
    tk = min(past, CACHE_KEY_TILE)
    new = pl.BlockSpec((1, n, d), lambda i, j: (i, 0, 0))
    old = pl.BlockSpec((1, tk, d), lambda i, j: (i, j, 0))
    return pl.pallas_call(
        functools.partial(_attn_sample_kernel, n=n),
        grid=(b, past // tk),
        in_specs=[new, new, new, old, old,
                  pl.BlockSpec((4, HEAD_DIM), lambda i, j: (0, 0)),
                  pl.BlockSpec((1, HEAD_W), lambda i, j: (0, 0))],
        out_specs=new,
        out_shape=jax.ShapeDtypeStruct((b, n, d), BF16),
        scratch_shapes=[
            pltpu.VMEM((N_HEADS, 2 * n, 1), F32),
            pltpu.VMEM((N_HEADS, 2 * n, 1), F32),
            pltpu.VMEM((N_HEADS, 2 * n, HEAD_W), F32),
        ],
        compiler_params=pltpu.CompilerParams(
            dimension_semantics=("arbitrary", "arbitrary"), vmem_limit_bytes=VMEM_LIMIT),
        name="attn_sample",
    )(qb, kb, vb, cache_k, cache_v, lam_p, subln)


def _outproj_kernel(x_ref, o_ref, wo_ref, y_ref):
    y_ref[...] = x_ref[...] + jnp.dot(o_ref[...], wo_ref[...], preferred_element_type=F32)


def _attn_outproj(x, ob, wo):
    t, d = x.shape
    tm = min(t, TOKEN_TILE)
    row = pl.BlockSpec((tm, d), lambda i: (i, 0))
    return pl.pallas_call(
        _outproj_kernel,
        grid=(t // tm,),
        in_specs=[row, row, pl.BlockSpec((d, d), lambda i: (0, 0))],
        out_specs=row,
        out_shape=jax.ShapeDtypeStruct((t, d), F32),
        compiler_params=pltpu.CompilerParams(
            dimension_semantics=("arbitrary",), vmem_limit_bytes=VMEM_LIMIT),
        name="attn_outproj",
    )(x, ob, wo)


def _trunk(x, conv_state, past_k, past_v, w):
    b, s, d = x.shape
    t = b * s
    x1, new_conv = _conv_layer(x, conv_state, w["g_mix0"], w["conv_w_in"], w["conv_w"], w["conv_w_out"])
    x2 = _peer_layer(x1.reshape(t, d), w["g_ffn0"], w["g_final"], *w["peer0"], final=False)
    k, v, kb, vb, qb = _kvq_proj(x2, w["g_kv"], w["g_mix1"], w["attn_wk"], w["attn_wv"], w["attn_wq"])
    shp = (b, s, d)
    if past_k is None:
        ob = _attn_prompt(qb.reshape(shp), kb.reshape(shp), vb.reshape(shp), w["attn_lambda"], w["attn_subln"])
    else:
        ob = _attn_sample(qb.reshape(shp), kb.reshape(shp), vb.reshape(shp),
                          past_k.reshape(b, -1, d), past_v.reshape(b, -1, d),
                          w["attn_lambda"], w["attn_subln"])
    x3 = _attn_outproj(x2, ob.reshape(t, d), w["attn_wo"])
    y = _peer_layer(x3, w["g_ffn1"], w["g_final"], *w["peer1"], final=True)
    return (y.reshape(shp), k.reshape(b, s, N_HEADS, 2, HEAD_DIM), v.reshape(b, s, N_HEADS, HEAD_W),
            new_conv[None])


def kernel(x_prompt, x_sample, cache_k, cache_v, state_conv, g_mix, g_ffn, conv_w_in, conv_w, conv_w_out, g_kv, attn_wk, attn_wv, attn_wq, attn_lambda, attn_subln, attn_wo, peer_wq, peer_keys, peer_u, peer_v, g_final):
    d = D_MODEL

    def peer_weights(l):
        return (peer_wq[l].T.astype(BF16),
                peer_keys[l].reshape(2 * PEER_HEADS, N_KEYS, N_KEYS).astype(BF16),
                peer_u[l].astype(BF16),
                peer_v[l].T.astype(BF16))

    w = {
        "g_mix0": g_mix[0].reshape(1, d), "g_mix1": g_mix[1].reshape(1, d),
        "g_ffn0": g_ffn[0].reshape(1, d), "g_ffn1": g_ffn[1].reshape(1, d),
        "g_kv": g_kv.reshape(1, d), "g_final": g_final.reshape(1, d),
        "conv_w_in": conv_w_in[0].astype(BF16), "conv_w": conv_w[0], "conv_w_out": conv_w_out[0].astype(BF16),
        "attn_wk": attn_wk.astype(BF16), "attn_wv": attn_wv.astype(BF16), "attn_wq": attn_wq[0].astype(BF16),
        "attn_lambda": attn_lambda[0], "attn_subln": attn_subln[0].reshape(1, HEAD_W),
        "attn_wo": attn_wo[0].astype(BF16),
        "peer0": peer_weights(0), "peer1": peer_weights(1),
    }
    zero_state = jnp.zeros((x_prompt.shape[0], CONV_W - 1, d), F32)
    y_p, k_p, v_p, conv_p = _trunk(x_prompt, zero_state, None, None, w)
    y_s, k_s, v_s, conv_s = _trunk(x_sample, state_conv[0], cache_k, cache_v, w)
    return (y_p, y_s, k_p, v_p, conv_p, k_s, v_s, conv_s)
```

```python
import functools
import math

import jax
import jax.numpy as jnp
from jax import lax
from jax.experimental import pallas as pl
from jax.experimental.pallas import tpu as pltpu

D_MODEL = 1024
CHUNK = 64
CONV_W = 3
HEAD_DIM = 64
N_HEADS = 8
HEAD_W = 2 * HEAD_DIM
PEER_HEADS = 8
N_KEYS = 128
N_EXPERTS = N_KEYS * N_KEYS
PEER_TOPK = 16
PEER_QDIM = 256
BW_SKEW = 16
EPS = 1e-6
LAMBDA_INIT_L1 = 0.8 - 0.6 * math.exp(-0.3 * 1)
QUERY_SCALE = HEAD_DIM ** -0.5 * math.log2(math.e)

LANES = 128
VMEM_LIMIT = 56 * 1024 * 1024
TOKEN_TILE = 512
PEER_STEP_EXPERTS = 2048
ATTN_Q_TILE = 512
CACHE_KEY_TILE = 1024

F32 = jnp.float32
BF16 = jnp.bfloat16
NEG_INF = float("-inf")


def _rms_rows(x, g):
    return x * lax.rsqrt(jnp.mean(x * x, axis=-1, keepdims=True) + EPS) * g


def _conv_kernel(x_ref, st_ref, g_ref, win_ref, cw_ref, wout_ref, o_ref, ns_ref, zp_ref, *, ts):
    s = pl.program_id(1)
    d = D_MODEL

    @pl.when(s == 0)
    def _():
        zp_ref[0:8, :] = jnp.zeros((8, d), F32)
        zp_ref[6:8, :] = st_ref[0]

    x = x_ref[0]
    h = _rms_rows(x, g_ref[...]).astype(BF16)
    p = jnp.dot(h, win_ref[...], preferred_element_type=F32)
    z = p[:, d:2 * d] * p[:, 2 * d:]
    zp_ref[8:8 + ts, :] = z
    conv = (cw_ref[0:1, :] * zp_ref[6:6 + ts, :] + cw_ref[1:2, :] * zp_ref[7:7 + ts, :]
            + cw_ref[2:3, :] * z)
    y = jnp.dot((p[:, :d] * conv).astype(BF16), wout_ref[...], preferred_element_type=F32)
    o_ref[0] = x + y
    ns_ref[0] = zp_ref[ts + 6:ts + 8, :]
    zp_ref[0:8, :] = zp_ref[ts:ts + 8, :]


def _conv_layer(x, state, g, w_in, conv_w, w_out):
    b, s, d = x.shape
    ts = min(s, TOKEN_TILE)
    return pl.pallas_call(
        functools.partial(_conv_kernel, ts=ts),
        grid=(b, s // ts),
        in_specs=[
            pl.BlockSpec((1, ts, d), lambda i, j: (i, j, 0)),
            pl.BlockSpec((1, CONV_W - 1, d), lambda i, j: (i, 0, 0)),
            pl.BlockSpec((1, d), lambda i, j: (0, 0)),
            pl.BlockSpec((d, 3 * d), lambda i, j: (0, 0)),
            pl.BlockSpec((CONV_W, d), lambda i, j: (0, 0)),
            pl.BlockSpec((d, d), lambda i, j: (0, 0)),
        ],
        out_specs=[
            pl.BlockSpec((1, ts, d), lambda i, j: (i, j, 0)),
            pl.BlockSpec((1, CONV_W - 1, d), lambda i, j: (i, 0, 0)),
        ],
        out_shape=[
            jax.ShapeDtypeStruct((b, s, d), F32),
            jax.ShapeDtypeStruct((b, CONV_W - 1, d), F32),
        ],
        scratch_shapes=[pltpu.VMEM((ts + 8, d), F32)],
        compiler_params=pltpu.CompilerParams(
            dimension_semantics=("arbitrary", "arbitrary"), vmem_limit_bytes=VMEM_LIMIT),
        name="conv_layer",
    )(x, state, g, w_in, conv_w, w_out)


def _sort_network(n):
    pairs = []
    p = 1
    while p < n:
        k = p
        while k >= 1:
            for j in range(k % p, n - k, 2 * k):
                for i in range(min(k, n - j - k)):
                    if (i + j) // (2 * p) == (i + j + k) // (2 * p):
                        pairs.append((i + j, i + j + k))
            k //= 2
        p *= 2
    return pairs


def _top16(s):
    sub = s.shape[0] // PEER_TOPK
    v = [s[sub * k:sub * (k + 1)] for k in range(PEER_TOPK)]
    for i, j in _sort_network(PEER_TOPK):
        v[i], v[j] = jnp.maximum(v[i], v[j]), jnp.minimum(v[i], v[j])
    shift = sub // 2
    while shift >= 1:
        v = [jnp.maximum(v[k], pltpu.roll(v[PEER_TOPK - 1 - k], shift, 0)) for k in range(PEER_TOPK)]
        d = PEER_TOPK // 2
        while d >= 1:
            for k in range(PEER_TOPK):
                if k & d == 0:
                    v[k], v[k + d] = jnp.maximum(v[k], v[k + d]), jnp.minimum(v[k], v[k + d])
            d //= 2
        shift //= 2
    return [t[0:1] for t in v]


def _peer_route(s0, s1):
    v0 = _top16(s0)
    v1 = _top16(s1)
    rank1 = jnp.full(s1.shape, float(PEER_TOPK), F32)
    for b in reversed(range(PEER_TOPK)):
        rank1 = jnp.where(s1 >= v1[b], float(b), rank1)
    v0a = jnp.concatenate(v0, axis=0)
    v1a = jnp.concatenate(v1, axis=0)
    cand = jnp.concatenate(
        [v0[0] + v1a] + [v0[a] + v1a[0:8] for a in range(1, 8)] + [v0a[8:16] + v1[0]], axis=0)
    top = v0[0] + v1[0]
    pad = jnp.full((s0.shape[0] - cand.shape[0], cand.shape[1]), NEG_INF, F32)
    tau = _top16(jnp.concatenate([cand, pad], axis=0))[PEER_TOPK - 1]
    z = jnp.sum(jnp.where(cand >= tau, jnp.exp(cand - top), 0.0), axis=0, keepdims=True)
    cnt = jnp.zeros(s0.shape, F32)
    low = 4
    for b in range(low):
        cnt = cnt + jnp.where(s0 + v1[b] >= tau, 1.0, 0.0)
    for a in range(PEER_TOPK // (low + 1)):
        hi = jnp.zeros(tau.shape, F32)
        for b in range(low, PEER_TOPK // (a + 1)):
            hi = hi + jnp.where(v0[a] + v1[b] >= tau, 1.0, 0.0)
        cnt = cnt + jnp.where(s0 == v0[a], hi, 0.0)
    a_w = jnp.exp(s0 - v0[0]) * (0.5 / z)
    b_w = jnp.exp(s1 - v1[0])
    return cnt, rank1, a_w, b_w


def _peer_kernel(x_ref, g_ref, gf_ref, wqt_ref, keys_ref, u_ref, vt_ref, o_ref,
                 ht_ref, sc_ref, rk_ref, bw_ref, cn_ref, aw_ref, acc_ref, a_ref, w_ref,
                 *, tm, ne, final):
    e = pl.program_id(1)
    n_e = pl.num_programs(1)
    rows_per_step = ne // N_KEYS
    n_chunks = tm // LANES

    @pl.when(e == 0)
    def _prologue():
        h = _rms_rows(x_ref[...], g_ref[...])
        ht_ref[...] = h.T.astype(BF16)
        qt = jnp.dot(wqt_ref[...], ht_ref[...], preferred_element_type=F32).astype(BF16)
        for hc in range(2 * PEER_HEADS):
            sc_ref[hc] = jnp.dot(keys_ref[hc], qt[hc * N_KEYS:(hc + 1) * N_KEYS],
                                 preferred_element_type=F32)

        def route_body(idx, carry):
            hd = idx // n_chunks
            ln = pl.ds(pl.multiple_of((idx % n_chunks) * LANES, LANES), LANES)
            cnt, rank1, a_w, b_w = _peer_route(sc_ref[2 * hd, :, ln], sc_ref[2 * hd + 1, :, ln])
            cn_ref[hd, :, ln] = cnt
            aw_ref[hd, :, ln] = a_w
            rk_ref[hd, idx % n_chunks] = rank1.astype(BF16)
            slab = pl.multiple_of(BW_SKEW + (hd * n_chunks + idx % n_chunks) * N_KEYS, 16)
            bw_ref[pl.ds(slab, N_KEYS), :] = b_w.astype(BF16)
            return carry

        lax.fori_loop(0, PEER_HEADS * n_chunks, route_body, 0)
        acc_ref[...] = jnp.zeros(acc_ref.shape, F32)

    a_ref[...] = jnp.dot(u_ref[...], ht_ref[...], preferred_element_type=F32)
    pack = (N_KEYS // 16, 16, LANES)
    for r in range(rows_per_step):
        i0 = pl.ds(pl.multiple_of(e * rows_per_step + (r // 8) * 8, 8), 8)
        rows = slice(r * N_KEYS, (r + 1) * N_KEYS)
        for tc in range(n_chunks):
            ln = slice(tc * LANES, (tc + 1) * LANES)
            gate = jnp.zeros(pack, BF16)
            for hd in range(PEER_HEADS):
                c_b = jnp.broadcast_to(cn_ref[hd, i0, ln][r % 8:r % 8 + 1], (16, LANES)).astype(BF16)
                a_b = jnp.broadcast_to(aw_ref[hd, i0, ln][r % 8:r % 8 + 1], (16, LANES)).astype(BF16)
                sel = jnp.where(rk_ref[hd, tc].reshape(pack) < c_b[None],
                                bw_ref[pl.ds(BW_SKEW + (hd * n_chunks + tc) * N_KEYS, N_KEYS), :].reshape(pack),
                                jnp.zeros((), BF16))
                gate = gate + sel * a_b[None]
            pre = a_ref[rows, ln]
            act = pre * (1.0 + lax.erf(pre * (2.0 ** -0.5)))
            w_ref[rows, ln] = gate.reshape(N_KEYS, LANES) * act.astype(BF16)
    acc_ref[...] += jnp.dot(vt_ref[...], w_ref[...], preferred_element_type=F32)

    @pl.when(e == n_e - 1)
    def _epilogue():
        y = x_ref[...] + acc_ref[...].T
        if final:
            y = _rms_rows(y, gf_ref[...])
        o_ref[...] = y


def _peer_layer(x, g, g_final, wq_t, keys, u, v_t, *, final):
    t, d = x.shape
    tm = min(t, TOKEN_TILE)
    ne = PEER_STEP_EXPERTS
    return pl.pallas_call(
        functools.partial(_peer_kernel, tm=tm, ne=ne, final=final),
        grid=(t // tm, N_EXPERTS // ne),
        in_specs=[
            pl.BlockSpec((tm, d), lambda i, e: (i, 0)),
            pl.BlockSpec((1, d), lambda i, e: (0, 0)),
            pl.BlockSpec((1, d), lambda i, e: (0, 0)),
            pl.BlockSpec((PEER_HEADS * PEER_QDIM, d), lambda i, e: (0, 0)),
            pl.BlockSpec((2 * PEER_HEADS, N_KEYS, N_KEYS), lambda i, e: (0, 0, 0)),
            pl.BlockSpec((ne, d), lambda i, e: (e, 0)),
            pl.BlockSpec((d, ne), lambda i, e: (0, e)),
        ],
        out_specs=pl.BlockSpec((tm, d), lambda i, e: (i, 0)),
        out_shape=jax.ShapeDtypeStruct((t, d), F32),
        scratch_shapes=[
            pltpu.VMEM((d, tm), BF16),
            pltpu.VMEM((2 * PEER_HEADS, N_KEYS, tm), F32),
            pltpu.VMEM((PEER_HEADS, tm // LANES, N_KEYS, LANES), BF16),
            pltpu.VMEM((BW_SKEW + PEER_HEADS * (tm // LANES) * N_KEYS, LANES), BF16),
            pltpu.VMEM((PEER_HEADS, N_KEYS, tm), F32),
            pltpu.VMEM((PEER_HEADS, N_KEYS, tm), F32),
            pltpu.VMEM((d, tm), F32),
            pltpu.VMEM((ne, tm), F32),
            pltpu.VMEM((ne, tm), BF16),
        ],
        compiler_params=pltpu.CompilerParams(
            dimension_semantics=("arbitrary", "arbitrary"), vmem_limit_bytes=VMEM_LIMIT),
        name="peer_final" if final else "peer",
    )(x, g, g_final, wq_t, keys, u, v_t)


def _kvq_kernel(x_ref, gkv_ref, gq_ref, wk_ref, wv_ref, wq_ref,
                k_ref, v_ref, kb_ref, vb_ref, qb_ref):
    x = x_ref[...]
    n = x * lax.rsqrt(jnp.mean(x * x, axis=-1, keepdims=True) + EPS)
    hk = (n * gkv_ref[...]).astype(BF16)
    hq = (n * gq_ref[...]).astype(BF16)
    k = jnp.dot(hk, wk_ref[...], preferred_element_type=F32)
    v = jnp.dot(hk, wv_ref[...], preferred_element_type=F32)
    q = jnp.dot(hq, wq_ref[...], preferred_element_type=F32)
    k_ref[...] = k
    v_ref[...] = v
    kb_ref[...] = k.astype(BF16)
    vb_ref[...] = v.astype(BF16)
    qb_ref[...] = (q * QUERY_SCALE).astype(BF16)


def _kvq_proj(x, g_kv, g_q, wk, wv, wq):
    t, d = x.shape
    tm = min(t, TOKEN_TILE)
    row = pl.BlockSpec((tm, d), lambda i: (i, 0))
    vec = pl.BlockSpec((1, d), lambda i: (0, 0))
    mat = pl.BlockSpec((d, d), lambda i: (0, 0))
    return pl.pallas_call(
        _kvq_kernel,
        grid=(t // tm,),
        in_specs=[row, vec, vec, mat, mat, mat],
        out_specs=[row] * 5,
        out_shape=[jax.ShapeDtypeStruct((t, d), F32)] * 2 + [jax.ShapeDtypeStruct((t, d), BF16)] * 3,
        compiler_params=pltpu.CompilerParams(
            dimension_semantics=("arbitrary",), vmem_limit_bytes=VMEM_LIMIT),
        name="kvq_proj",
    )(x, g_kv, g_q, wk, wv, wq)


def _lambda_full(lam_ref):
    lp = lam_ref[...]
    s01 = jnp.sum(lp[0:1] * lp[1:2], axis=-1, keepdims=True)
    s23 = jnp.sum(lp[2:3] * lp[3:4], axis=-1, keepdims=True)
    return jnp.exp(s01) - jnp.exp(s23) + LAMBDA_INIT_L1


def _stack_maps(q):
    lane = lax.broadcasted_iota(jnp.int32, q.shape, 1)
    zero = jnp.zeros((), q.dtype)
    return jnp.concatenate([jnp.where(lane < HEAD_DIM, q, zero), jnp.where(lane >= HEAD_DIM, q, zero)],
                           axis=0)


def _flash_step(qs, k, v, m, l, acc, mask):
    s = lax.dot_general(qs, k, (((1,), (1,)), ((), ())), preferred_element_type=F32)
    if mask is not None:
        s = jnp.where(mask, s, NEG_INF)
    m_new = jnp.maximum(m, jnp.max(s, axis=-1, keepdims=True))
    alpha = jnp.exp2(m - m_new)
    p = jnp.exp2(s - m_new)
    l = alpha * l + jnp.sum(p, axis=-1, keepdims=True)
    acc = alpha * acc + jnp.dot(p.astype(BF16), v, preferred_element_type=F32)
    return m_new, l, acc


def _diff_finish(l, acc, lam, subln, n):
    o = acc / l
    o = o[:n] - lam * o[n:]
    o = o * lax.rsqrt(jnp.mean(o * o, axis=-1, keepdims=True) + EPS) * subln
    return (o * (1.0 - LAMBDA_INIT_L1)).astype(BF16)


def _attn_prompt_kernel(q_ref, k_ref, v_ref, lam_ref, sub_ref, o_ref,
                        vt_ref, qt_ref, sa_ref, sb_ref, acc_ref, *, tq, tk):
    qi = pl.program_id(2)
    n2 = 2 * tq
    seq = v_ref.shape[1]

    @pl.when(qi == 0)
    def _():
        for c in range(seq // tq):
            rows = slice(c * tq, (c + 1) * tq)
            vt_ref[:, rows] = v_ref[0, rows, :].astype(F32).T.astype(BF16)

    qt_ref[...] = _stack_maps(q_ref[0]).astype(F32).T.astype(BF16)
    acc_ref[...] = jnp.zeros(acc_ref.shape, F32)

    def scores(t):
        rows = pl.ds(pl.multiple_of(t * tk, tk), tk)
        return jnp.dot(k_ref[0, rows, :], qt_ref[...], preferred_element_type=F32)

    def consume(s, t, m, l):
        cols = pl.ds(pl.multiple_of(t * tk, tk), tk)
        m_new = jnp.maximum(m, jnp.max(s, axis=0, keepdims=True))
        alpha = jnp.exp2(m - m_new)
        p = jnp.exp2(s - m_new)
        l = alpha * l + jnp.sum(p, axis=0, keepdims=True)
        acc_ref[...] = alpha * acc_ref[...] + jnp.dot(vt_ref[:, cols], p.astype(BF16),
                                                      preferred_element_type=F32)
        return m_new, l

    def pair(p, carry):
        m, l = carry
        sb_ref[...] = scores(2 * p + 1)
        m, l = consume(sa_ref[...], 2 * p, m, l)
        sa_ref[...] = scores(2 * p + 2)
        return consume(sb_ref[...], 2 * p + 1, m, l)

    def diag_bias(s, half):
        qc = (lax.broadcasted_iota(jnp.int32, (1, n2), 1) % tq) // CHUNK
        parts = []
        for c in range(tk // CHUNK):
            kc = half * (tk // CHUNK) + c
            bias = jnp.where(qc >= kc, 0.0, NEG_INF)
            parts.append(s[c * CHUNK:(c + 1) * CHUNK] + bias)
        return jnp.concatenate(parts, axis=0)

    sa_ref[...] = scores(0)
    m0 = jnp.full((1, n2), NEG_INF, F32)
    l0 = jnp.zeros((1, n2), F32)
    m, l = lax.fori_loop(0, qi, pair, (m0, l0))
    t0 = qi * (tq // tk)
    sb_ref[...] = scores(t0 + 1)
    m, l = consume(diag_bias(sa_ref[...], 0), t0, m, l)
    m, l = consume(diag_bias(sb_ref[...], 1), t0 + 1, m, l)

    o = acc_ref[...] * (1.0 / l)
    o = o[:, :tq] - _lambda_full(lam_ref) * o[:, tq:]
    o = o * lax.rsqrt(jnp.mean(o * o, axis=0, keepdims=True) + EPS)
    o_ref[0] = (o.T * sub_ref[...] * (1.0 - LAMBDA_INIT_L1)).astype(BF16)


def _attn_prompt(qb, kb, vb, lam_p, subln):
    b, s, d = qb.shape
    tq = ATTN_Q_TILE
    tk = tq // 2
    return pl.pallas_call(
        functools.partial(_attn_prompt_kernel, tq=tq, tk=tk),
        grid=(b, N_HEADS, s // tq),
        in_specs=[
            pl.BlockSpec((1, tq, HEAD_W), lambda i, h, j: (i, j, h)),
            pl.BlockSpec((1, s, HEAD_W), lambda i, h, j: (i, 0, h)),
            pl.BlockSpec((1, s, HEAD_W), lambda i, h, j: (i, 0, h)),
            pl.BlockSpec((4, HEAD_DIM), lambda i, h, j: (0, 0)),
            pl.BlockSpec((1, HEAD_W), lambda i, h, j: (0, 0)),
        ],
        out_specs=pl.BlockSpec((1, tq, HEAD_W), lambda i, h, j: (i, j, h)),
        out_shape=jax.ShapeDtypeStruct((b, s, d), BF16),
        scratch_shapes=[
            pltpu.VMEM((HEAD_W, s), BF16),
            pltpu.VMEM((HEAD_W, 2 * tq), BF16),
            pltpu.VMEM((tk, 2 * tq), F32),
            pltpu.VMEM((tk, 2 * tq), F32),
            pltpu.VMEM((HEAD_W, 2 * tq), F32),
        ],
        compiler_params=pltpu.CompilerParams(
            dimension_semantics=("arbitrary", "arbitrary", "arbitrary"), vmem_limit_bytes=VMEM_LIMIT),
        name="attn_prompt",
    )(qb, kb, vb, lam_p, subln)


def _attn_sample_kernel(q_ref, kn_ref, vn_ref, ck_ref, cv_ref, lam_ref, sub_ref, o_ref,
                        m_ref, l_ref, acc_ref, *, n):
    j = pl.program_id(1)
    last = pl.num_programs(1) - 1

    @pl.when(j == 0)
    def _():
        m_ref[...] = jnp.full(m_ref.shape, NEG_INF, F32)
        l_ref[...] = jnp.zeros(l_ref.shape, F32)
        acc_ref[...] = jnp.zeros(acc_ref.shape, F32)

    for h in range(N_HEADS):
        ln = slice(h * HEAD_W, (h + 1) * HEAD_W)
        qs = _stack_maps(q_ref[0, :, ln])
        m, l, acc = _flash_step(qs, ck_ref[0, :, ln].astype(BF16), cv_ref[0, :, ln].astype(BF16),
                                m_ref[h], l_ref[h], acc_ref[h], None)
        m_ref[h] = m
        l_ref[h] = l
        acc_ref[h] = acc

    @pl.when(j == last)
    def _():
        lam = _lambda_full(lam_ref)
        for h in range(N_HEADS):
            ln = slice(h * HEAD_W, (h + 1) * HEAD_W)
            qs = _stack_maps(q_ref[0, :, ln])
            m, l, acc = _flash_step(qs, kn_ref[0, :, ln], vn_ref[0, :, ln],
                                    m_ref[h], l_ref[h], acc_ref[h], None)
            o_ref[0, :, ln] = _diff_finish(l, acc, lam, sub_ref[...], n)


def _attn_sample(qb, kb, vb, cache_k, cache_v, lam_p, subln):
    b, n, d = qb.shape
    past = cache_k.shape[1]
    tk = min(past, CACHE_KEY_TILE)
    new = pl.BlockSpec((1, n, d), lambda i, j: (i, 0, 0))
    old = pl.BlockSpec((1, tk, d), lambda i, j: (i, j, 0))
    return pl.pallas_call(
        functools.partial(_attn_sample_kernel, n=n),
        grid=(b, past // tk),
        in_specs=[new, new, new, old, old,
                  pl.BlockSpec((4, HEAD_DIM), lambda i, j: (0, 0)),
                  pl.BlockSpec((1, HEAD_W), lambda i, j: (0, 0))],
        out_specs=new,
        out_shape=jax.ShapeDtypeStruct((b, n, d), BF16),
        scratch_shapes=[
            pltpu.VMEM((N_HEADS, 2 * n, 1), F32),
            pltpu.VMEM((N_HEADS, 2 * n, 1), F32),
            pltpu.VMEM((N_HEADS, 2 * n, HEAD_W), F32),
        ],
        compiler_params=pltpu.CompilerParams(
            dimension_semantics=("arbitrary", "arbitrary"), vmem_limit_bytes=VMEM_LIMIT),
        name="attn_sample",
    )(qb, kb, vb, cache_k, cache_v, lam_p, subln)


def _outproj_kernel(x_ref, o_ref, wo_ref, y_ref):
    y_ref[...] = x_ref[...] + jnp.dot(o_ref[...], wo_ref[...], preferred_element_type=F32)


def _attn_outproj(x, ob, wo):
    t, d = x.shape
    tm = min(t, TOKEN_TILE)
    row = pl.BlockSpec((tm, d), lambda i: (i, 0))
    return pl.pallas_call(
        _outproj_kernel,
        grid=(t // tm,),
        in_specs=[row, row, pl.BlockSpec((d, d), lambda i: (0, 0))],
        out_specs=row,
        out_shape=jax.ShapeDtypeStruct((t, d), F32),
        compiler_params=pltpu.CompilerParams(
            dimension_semantics=("arbitrary",), vmem_limit_bytes=VMEM_LIMIT),
        name="attn_outproj",
    )(x, ob, wo)


def _trunk(x, conv_state, past_k, past_v, w):
    b, s, d = x.shape
    t = b * s
    x1, new_conv = _conv_layer(x, conv_state, w["g_mix0"], w["conv_w_in"], w["conv_w"], w["conv_w_out"])
    x2 = _peer_layer(x1.reshape(t, d), w["g_ffn0"], w["g_final"], *w["peer0"], final=False)
    k, v, kb, vb, qb = _kvq_proj(x2, w["g_kv"], w["g_mix1"], w["attn_wk"], w["attn_wv"], w["attn_wq"])
    shp = (b, s, d)
    if past_k is None:
        ob = _attn_prompt(qb.reshape(shp), kb.reshape(shp), vb.reshape(shp), w["attn_lambda"], w["attn_subln"])
    else:
        ob = _attn_sample(qb.reshape(shp), kb.reshape(shp), vb.reshape(shp),
                          past_k.reshape(b, -1, d), past_v.reshape(b, -1, d),
                          w["attn_lambda"], w["attn_subln"])
    x3 = _attn_outproj(x2, ob.reshape(t, d), w["attn_wo"])
    y = _peer_layer(x3, w["g_ffn1"], w["g_final"], *w["peer1"], final=True)
    return (y.reshape(shp), k.reshape(b, s, N_HEADS, 2, HEAD_DIM), v.reshape(b, s, N_HEADS, HEAD_W),
            new_conv[None])


def kernel(x_prompt, x_sample, cache_k, cache_v, state_conv, g_mix, g_ffn, conv_w_in, conv_w, conv_w_out, g_kv, attn_wk, attn_wv, attn_wq, attn_lambda, attn_subln, attn_wo, peer_wq, peer_keys, peer_u, peer_v, g_final):
    d = D_MODEL

    def peer_weights(l):
        return (peer_wq[l].T.astype(BF16),
                peer_keys[l].reshape(2 * PEER_HEADS, N_KEYS, N_KEYS).astype(BF16),
                peer_u[l].astype(BF16),
                peer_v[l].T.astype(BF16))

    w = {
        "g_mix0": g_mix[0].reshape(1, d), "g_mix1": g_mix[1].reshape(1, d),
        "g_ffn0": g_ffn[0].reshape(1, d), "g_ffn1": g_ffn[1].reshape(1, d),
        "g_kv": g_kv.reshape(1, d), "g_final": g_final.reshape(1, d),
        "conv_w_in": conv_w_in[0].astype(BF16), "conv_w": conv_w[0], "conv_w_out": conv_w_out[0].astype(BF16),
        "attn_wk": attn_wk.astype(BF16), "attn_wv": attn_wv.astype(BF16), "attn_wq": attn_wq[0].astype(BF16),
        "attn_lambda": attn_lambda[0], "attn_subln": attn_subln[0].reshape(1, HEAD_W),
        "attn_wo": attn_wo[0].astype(BF16),
        "peer0": peer_weights(0), "peer1": peer_weights(1),
    }
    zero_state = jnp.zeros((x_prompt.shape[0], CONV_W - 1, d), F32)
    y_p, k_p, v_p, conv_p = _trunk(x_prompt, zero_state, None, None, w)
    y_s, k_s, v_s, conv_s = _trunk(x_sample, state_conv[0], cache_k, cache_v, w)
    return (y_p, y_s, k_p, v_p, conv_p, k_s, v_s, conv_s)
```
